```python
import jax, jax.numpy as jnp
from jax import lax
import numpy as np

D_MODEL = 1024
BATCH = 16
SEQ = 4096
DEPTH = 1
DEC_BATCH = 2
DEC_SEQ = 8192
PAST_LEN = 128

HEAD_DIM = 64
A_Q_HEADS = 8
A_KV_HEADS = 2
A_GROUP = A_Q_HEADS // A_KV_HEADS
B_WIN_DIL = ((128, 1), (512, 4), (2048, 16))
B_N_GROUPS = 3
B_HEADS_PER_GROUP = 4
B_HEADS = B_N_GROUPS * B_HEADS_PER_GROUP
D_FF = 2816
GRID_W = 64
Q_BLOCK = 128
ROPE_THETA = 500000.0
AXIAL_THETA = 10000.0
ROT_DIMS = HEAD_DIM // 4
EPS = 1e-6
NEG_INF = -1e30
A_Q_W = A_Q_HEADS * HEAD_DIM
A_KV_W = A_KV_HEADS * HEAD_DIM
B_W = B_HEADS * HEAD_DIM
B_OUT_W = B_HEADS_PER_GROUP * HEAD_DIM
IN_W = A_Q_W + 2 * A_KV_W + 3 * B_W + 2 * D_MODEL

kernel_name = "hybrid_gated_gqa_dilated_encoder"


def rms_norm(x, g):
    xf = x.astype(jnp.float32)
    y = xf * lax.rsqrt(jnp.mean(xf * xf, axis=-1, keepdims=True) + EPS)
    return (y * g.astype(jnp.float32)).astype(x.dtype)


def rope(x, pos, theta):
    r = x.shape[-1]
    inv = theta ** (-jnp.arange(0, r, 2, dtype=jnp.float32) / r)
    ang = pos[:, None] * inv[None, :]
    cos = jnp.cos(ang)[None, :, None, :]
    sin = jnp.sin(ang)[None, :, None, :]
    xf = x.astype(jnp.float32)
    x1, x2 = xf[..., : r // 2], xf[..., r // 2:]
    out = jnp.concatenate([x1 * cos - x2 * sin, x2 * cos + x1 * sin], axis=-1)
    return out.astype(x.dtype)


def partial_rope(x, pos):
    return jnp.concatenate([rope(x[..., :ROT_DIMS], pos, ROPE_THETA), x[..., ROT_DIMS:]], axis=-1)


def axial_rope(x, row, col):
    half = HEAD_DIM // 2
    return jnp.concatenate([rope(x[..., :half], row, AXIAL_THETA),
                            rope(x[..., half:], col, AXIAL_THETA)], axis=-1)


def swiglu(x, w13, w2):
    a, b = jnp.split(x @ w13, 2, axis=-1)
    return (jax.nn.silu(a) * b) @ w2


def grid_gqa(q, k, v):
    bsz, s_len = q.shape[:2]
    nb = s_len // Q_BLOCK
    scale = HEAD_DIM ** -0.5
    qg = q.reshape(bsz, s_len, A_KV_HEADS, A_GROUP, HEAD_DIM)

    def blk(i):
        qb = lax.dynamic_slice_in_dim(qg, i * Q_BLOCK, Q_BLOCK, axis=1)
        s = jnp.einsum('bqkgd,bskd->bkgqs', qb, k).astype(jnp.float32) * scale
        p = jax.nn.softmax(s, axis=-1).astype(v.dtype)
        return jnp.einsum('bkgqs,bskd->bqkgd', p, v)

    o = lax.map(blk, jnp.arange(nb))
    return jnp.moveaxis(o, 0, 1).reshape(bsz, s_len, A_Q_W)


def dilated_mix(q, k, v):
    bsz, s_len = q.shape[:2]
    nb = s_len // Q_BLOCK
    scale = HEAD_DIM ** -0.5
    shp = (bsz, s_len, B_N_GROUPS, B_HEADS_PER_GROUP, HEAD_DIM)
    qg, kg_all, vg_all = q.reshape(shp), k.reshape(shp), v.reshape(shp)

    def blk(i):
        pos = i * Q_BLOCK + jnp.arange(Q_BLOCK)
        qb = lax.dynamic_slice_in_dim(qg, i * Q_BLOCK, Q_BLOCK, axis=1)
        outs, lses = [], []
        for g, (win, dil) in enumerate(B_WIN_DIL):
            n_side = win // (2 * dil)
            offs = dil * jnp.arange(-n_side, n_side + 1)
            idx = pos[:, None] + offs[None, :]
            valid = (idx >= 0) & (idx < s_len)
            idx = jnp.clip(idx, 0, s_len - 1)
            kg = jnp.take(kg_all[:, :, g], idx, axis=1)
            vg = jnp.take(vg_all[:, :, g], idx, axis=1)
            s = jnp.einsum('bqhd,bqjhd->bqhj', qb[:, :, g], kg).astype(jnp.float32) * scale
            s = jnp.where(valid[None, :, None, :], s, NEG_INF)
            lse = jax.nn.logsumexp(s, axis=-1, keepdims=True)
            p = jnp.exp(s - lse).astype(v.dtype)
            outs.append(jnp.einsum('bqhj,bqjhd->bqhd', p, vg))
            lses.append(lse[..., 0])
        wts = jax.nn.softmax(jnp.stack(lses, axis=0), axis=0).astype(v.dtype)
        return jnp.einsum('gbqh,gbqhd->bqhd', wts, jnp.stack(outs, axis=0))

    o = lax.map(blk, jnp.arange(nb))
    return jnp.moveaxis(o, 0, 1).reshape(bsz, s_len, B_OUT_W)


def encoder_layer(x, n1, w13_1, w2_1, nm, w_in, qn, kn, w_ba, w_bb, w_o, n2, w13_2, w2_2):
    bsz, s_len, _ = x.shape
    x = x + 0.5 * swiglu(rms_norm(x, n1), w13_1, w2_1)
    h = rms_norm(x, nm)
    proj = h @ w_in
    cuts = np.cumsum([A_Q_W, A_KV_W, A_KV_W, B_W, B_W, B_W, D_MODEL])
    qa, ka, va, qb, kb, vb, ga, gb = jnp.split(proj, cuts, axis=-1)
    rows = s_len // GRID_W
    t = jnp.arange(s_len, dtype=jnp.float32)
    row = jnp.repeat(jnp.arange(rows, dtype=jnp.float32), GRID_W)
    col = jnp.tile(jnp.arange(GRID_W, dtype=jnp.float32), rows)
    qa = axial_rope(rms_norm(qa.reshape(bsz, s_len, A_Q_HEADS, HEAD_DIM), qn), row, col)
    ka = axial_rope(rms_norm(ka.reshape(bsz, s_len, A_KV_HEADS, HEAD_DIM), kn), row, col)
    va = va.reshape(bsz, s_len, A_KV_HEADS, HEAD_DIM)
    out_a = grid_gqa(qa, ka, va) @ w_ba
    qb = partial_rope(qb.reshape(bsz, s_len, B_HEADS, HEAD_DIM), t)
    kb = partial_rope(kb.reshape(bsz, s_len, B_HEADS, HEAD_DIM), t)
    vb = vb.reshape(bsz, s_len, B_HEADS, HEAD_DIM)
    out_b = dilated_mix(qb, kb, vb) @ w_bb
    merged = jax.nn.sigmoid(ga) * out_a + jax.nn.sigmoid(gb) * out_b
    x = x + merged @ w_o
    x = x + 0.5 * swiglu(rms_norm(x, n2), w13_2, w2_2)
    return x


def setup_inputs(seed: int = 0) -> dict:
    key = jax.random.key(seed)
    ks = jax.random.split(key, 20)

    def w(k, shape, fan_in):
        return jax.random.normal(k, shape, jnp.float32) * fan_in ** -0.5

    def gain(k, shape):
        return 1.0 + 0.02 * jax.random.normal(k, shape, jnp.float32)

    L = DEPTH
    return {
        "x_prompt": jax.random.normal(ks[0], (BATCH, SEQ, D_MODEL), jnp.float32),
        "x_sample": jax.random.normal(ks[1], (DEC_BATCH, DEC_SEQ, D_MODEL), jnp.float32),
        "norm_ffn1": gain(ks[2], (L, D_MODEL)),
        "w13_ffn1": w(ks[3], (L, D_MODEL, 2 * D_FF), D_MODEL),
        "w2_ffn1": w(ks[4], (L, D_FF, D_MODEL), D_FF),
        "norm_mix": gain(ks[5], (L, D_MODEL)),
        "w_in": w(ks[6], (L, D_MODEL, IN_W), D_MODEL),
        "q_norm_a": gain(ks[7], (L, HEAD_DIM)),
        "k_norm_a": gain(ks[8], (L, HEAD_DIM)),
        "w_branch_a": w(ks[9], (L, A_Q_W, D_MODEL), A_Q_W),
        "w_branch_b": w(ks[10], (L, B_OUT_W, D_MODEL), B_OUT_W),
        "w_out": w(ks[11], (L, D_MODEL, D_MODEL), D_MODEL),
        "norm_ffn2": gain(ks[12], (L, D_MODEL)),
        "w13_ffn2": w(ks[13], (L, D_MODEL, 2 * D_FF), D_MODEL),
        "w2_ffn2": w(ks[14], (L, D_FF, D_MODEL), D_FF),
        "norm_final": gain(ks[15], (D_MODEL,)),
    }


def reference(x_prompt, x_sample, norm_ffn1, w13_ffn1, w2_ffn1, norm_mix, w_in, q_norm_a,
              k_norm_a, w_branch_a, w_branch_b, w_out, norm_ffn2, w13_ffn2, w2_ffn2, norm_final):
    def trunk(x):
        for l in range(DEPTH):
            x = encoder_layer(x, norm_ffn1[l], w13_ffn1[l], w2_ffn1[l], norm_mix[l], w_in[l],
                              q_norm_a[l], k_norm_a[l], w_branch_a[l], w_branch_b[l], w_out[l],
                              norm_ffn2[l], w13_ffn2[l], w2_ffn2[l])
        return rms_norm(x, norm_final)

    y_prompt = trunk(x_prompt)
    y_sample = trunk(x_sample)
    return (y_prompt, y_sample)
```

```python
import functools

import jax
import jax.numpy as jnp
from jax import lax
from jax.experimental import pallas as pl
from jax.experimental.pallas import tpu as pltpu

D_MODEL = 1024
HEAD_DIM = 64
A_Q_HEADS = 8
A_KV_HEADS = 2
B_WIN_DIL = ((128, 1), (512, 4), (2048, 16))
B_N_GROUPS = 3
B_HEADS = 12
D_FF = 2816
GRID_W = 64
ROPE_THETA = 500000.0
AXIAL_THETA = 10000.0
ROT_DIMS = HEAD_DIM // 4
EPS = 1e-6
NEG_INF = -1e30
A_Q_W = A_Q_HEADS * HEAD_DIM
A_KV_W = A_KV_HEADS * HEAD_DIM
B_W = B_HEADS * HEAD_DIM
B_OUT_W = 4 * HEAD_DIM
QKV_W = A_Q_W + 2 * A_KV_W + 3 * B_W
B_SIDE = 64

LANES = 128
VMEM_LIMIT = 56 * 1024 * 1024

BF16 = jnp.bfloat16
F32 = jnp.float32

FFN_TM = 512
FFN_TF = 256
PROJ_TM = 512
MIX_TM = 512
B_TQ = 128
B_TK = B_TQ + 2 * B_SIDE
B_CHUNK = 1024


def _dot(a, b):
    return jnp.dot(a, b, preferred_element_type=F32)


def _dot_nt(a, b):
    return lax.dot_general(a, b, (((1,), (1,)), ((), ())), preferred_element_type=F32)


def _rms(x, g):
    ms = jnp.mean(x * x, axis=-1, keepdims=True)
    return x * lax.rsqrt(ms + EPS) * g


def _resident(shape):
    nd = len(shape)
    return pl.BlockSpec(shape, lambda *_: (0,) * nd, pipeline_mode=pl.Buffered(1))


def _params(n_axes):
    return pltpu.CompilerParams(
        dimension_semantics=("arbitrary",) * n_axes, vmem_limit_bytes=VMEM_LIMIT)


def _ffn_kernel(*refs, final_norm):
    if final_norm:
        x_ref, g_ref, w13_ref, w2_ref, gf_ref, o_ref = refs
    else:
        x_ref, g_ref, w13_ref, w2_ref, o_ref = refs
    x = x_ref[...]
    h = _rms(x, g_ref[...]).astype(BF16)
    y = None
    for c in range(D_FF // FFN_TF):
        lo, hi = c * FFN_TF, (c + 1) * FFN_TF
        a = _dot(h, w13_ref[:, lo:hi])
        b = _dot(h, w13_ref[:, D_FF + lo:D_FF + hi])
        t = (a * jax.nn.sigmoid(a) * b).astype(BF16)
        d = _dot(t, w2_ref[lo:hi, :])
        y = d if y is None else y + d
    out = x + 0.5 * y
    if final_norm:
        out = _rms(out, gf_ref[...])
    o_ref[...] = out


def _ffn(x2d, g, w13, w2, gf=None):
    m = x2d.shape[0]
    final_norm = gf is not None
    row = pl.BlockSpec((FFN_TM, D_MODEL), lambda i: (i, 0))
    in_specs = [row, _resident((1, D_MODEL)), _resident((D_MODEL, 2 * D_FF)),
                _resident((D_FF, D_MODEL))]
    args = [x2d, g, w13, w2]
    if final_norm:
        in_specs.append(_resident((1, D_MODEL)))
        args.append(gf)
    return pl.pallas_call(
        functools.partial(_ffn_kernel, final_norm=final_norm),
        grid=(m // FFN_TM,),
        in_specs=in_specs,
        out_specs=row,
        out_shape=jax.ShapeDtypeStruct((m, D_MODEL), F32),
        compiler_params=_params(1),
        name="ffn_final" if final_norm else "ffn",
    )(*args)


def _swap_halves(y, half):
    lane = lax.broadcasted_iota(jnp.int32, y.shape, 1)
    up = pltpu.roll(y, LANES - half, 1)
    dn = pltpu.roll(y, half, 1)
    return jnp.where((lane & half) == 0, up, dn)


def _rope(y, cos, sin, half):
    return y * cos + _swap_halves(y, half) * sin


def _proj_kernel(x_ref, nm_ref, w_ref, qn_ref, kn_ref, ca_ref, sa_ref, cb_ref, sb_ref,
                 qa_ref, kt_ref, va_ref, qb_ref, kb_ref, vb_ref):
    h = _rms(x_ref[0], nm_ref[...]).astype(BF16)
    ca, sa, cb, sb = ca_ref[...], sa_ref[...], cb_ref[...], sb_ref[...]

    r = lax.broadcasted_iota(jnp.int32, (LANES, LANES), 0) // HEAD_DIM
    c = lax.broadcasted_iota(jnp.int32, (LANES, LANES), 1) // HEAD_DIM
    head_ones = jnp.where(r == c, 1.0, 0.0).astype(BF16)

    def head_norm_rope(xp, gain):
        sq = xp * xp
        sq_hi = sq.astype(BF16)
        sq_lo = (sq - sq_hi.astype(F32)).astype(BF16)
        ss = _dot(sq_hi, head_ones) + _dot(sq_lo, head_ones)
        y = xp * lax.rsqrt(ss * (1.0 / HEAD_DIM) + EPS) * gain
        return _rope(y, ca, sa, HEAD_DIM // 4)

    scale = HEAD_DIM ** -0.5
    for j in range(A_Q_W // LANES):
        xp = _dot(h, w_ref[:, j * LANES:(j + 1) * LANES])
        qa_ref[0, :, j * LANES:(j + 1) * LANES] = (
            head_norm_rope(xp, qn_ref[...]) * scale).astype(BF16)

    off = A_Q_W
    k = head_norm_rope(_dot(h, w_ref[:, off:off + A_KV_W]), kn_ref[...])
    kt = k.T.astype(BF16)
    for kv in range(A_KV_HEADS):
        one = kt[kv * HEAD_DIM:(kv + 1) * HEAD_DIM]
        kt_ref[0, kv] = jnp.concatenate([one, one], axis=0)
    off += A_KV_W
    va_ref[0] = _dot(h, w_ref[:, off:off + A_KV_W]).astype(BF16)
    off += A_KV_W

    for j in range(B_W // LANES):
        xp = _dot(h, w_ref[:, off + j * LANES:off + (j + 1) * LANES])
        qb_ref[0, :, j * LANES:(j + 1) * LANES] = (
            _rope(xp, cb, sb, ROT_DIMS // 2) * scale).astype(BF16)
    off += B_W
    for j in range(B_W // LANES):
        xp = _dot(h, w_ref[:, off + j * LANES:off + (j + 1) * LANES])
        kb_ref[0, :, j * LANES:(j + 1) * LANES] = _rope(xp, cb, sb, ROT_DIMS // 2).astype(BF16)
    off += B_W
    vb_ref[0] = _dot(h, w_ref[:, off:off + B_W]).astype(BF16)


def _proj(x, nm, wqkv, qn, kn, tabs):
    bsz, s_len, _ = x.shape
    tm = PROJ_TM
    tab = pl.BlockSpec((tm, LANES), lambda b, j: (j, 0))

    def tok(w):
        return pl.BlockSpec((1, tm, w), lambda b, j: (b, j, 0))

    def out(w):
        return jax.ShapeDtypeStruct((bsz, s_len, w), BF16)

    return pl.pallas_call(
        _proj_kernel,
        grid=(bsz, s_len // tm),
        in_specs=[tok(D_MODEL), _resident((1, D_MODEL)), _resident((D_MODEL, QKV_W)),
                  _resident((1, LANES)), _resident((1, LANES)), tab, tab, tab, tab],
        out_specs=[tok(A_Q_W),
                   pl.BlockSpec((1, A_KV_HEADS, LANES, tm), lambda b, j: (b, 0, 0, j)),
                   tok(A_KV_W), tok(B_W), tok(B_W), tok(B_W)],
        out_shape=[out(A_Q_W),
                   jax.ShapeDtypeStruct((bsz, A_KV_HEADS, LANES, s_len), BF16),
                   out(A_KV_W), out(B_W), out(B_W), out(B_W)],
        compiler_params=_params(2),
        name="proj",
    )(x, nm, wqkv, qn, kn, *tabs)


def _attn_a_kernel(q_ref, kt_ref, v_ref, o_ref, *, tq):
    lane = lax.broadcasted_iota(jnp.int32, (tq, LANES), 1)
    low = lane < HEAD_DIM
    v = v_ref[0]
    outs = []
    for kv in range(A_KV_HEADS):
        k2 = kt_ref[0, kv]
        rows = []
        for pair in range(2):
            c0 = kv * 4 * HEAD_DIM + pair * LANES
            q2 = q_ref[0, :, c0:c0 + LANES]
            zero = jnp.zeros_like(q2)
            rows.append(jnp.where(low, q2, zero))
            rows.append(jnp.where(low, zero, q2))
        s = _dot(jnp.concatenate(rows, axis=0), k2)
        m = jnp.max(s, axis=-1, keepdims=True)
        p = jnp.exp(s - m)
        l = jnp.sum(p, axis=-1, keepdims=True)
        o = _dot(p.astype(BF16), v) / l
        for pair in range(2):
            a = o[(2 * pair) * tq:(2 * pair + 1) * tq]
            b = o[(2 * pair + 1) * tq:(2 * pair + 2) * tq]
            if kv == 0:
                b = pltpu.roll(b, HEAD_DIM, 1)
            else:
                a = pltpu.roll(a, HEAD_DIM, 1)
            outs.append(jnp.where(low, a, b))
    o_ref[0] = jnp.concatenate(outs, axis=1).astype(BF16)


def _attn_a(qa, kt, va):
    bsz, s_len, _ = qa.shape
    tq = 128 if s_len <= 4096 else 64
    return pl.pallas_call(
        functools.partial(_attn_a_kernel, tq=tq),
        grid=(bsz, s_len // tq),
        in_specs=[pl.BlockSpec((1, tq, A_Q_W), lambda b, i: (b, i, 0)),
                  pl.BlockSpec((1, A_KV_HEADS, LANES, s_len), lambda b, i: (b, 0, 0, 0)),
                  pl.BlockSpec((1, s_len, A_KV_W), lambda b, i: (b, 0, 0))],
        out_specs=pl.BlockSpec((1, tq, A_Q_W), lambda b, i: (b, i, 0)),
        out_shape=jax.ShapeDtypeStruct((bsz, s_len, A_Q_W), BF16),
        compiler_params=_params(2),
        name="attn_a",
    )(qa, kt, va)


def _attn_b_kernel(q_ref, kp_ref, kc_ref, kn_ref, vp_ref, vc_ref, vn_ref, o_ref, lse_ref,
                   kw_ref, vw_ref, *, n_rows, chunk):
    for w_ref, p_ref, c_ref, n_ref in ((kw_ref, kp_ref, kc_ref, kn_ref),
                                       (vw_ref, vp_ref, vc_ref, vn_ref)):
        w_ref[0:B_SIDE] = p_ref[0]
        w_ref[B_SIDE:B_SIDE + chunk] = c_ref[0]
        w_ref[B_SIDE + chunk:] = n_ref[0]

    lane = lax.broadcasted_iota(jnp.int32, (B_TQ, LANES), 1)
    low = lane < HEAD_DIM
    qi = lax.broadcasted_iota(jnp.int32, (B_TQ, B_TK), 0)
    ki = lax.broadcasted_iota(jnp.int32, (B_TQ, B_TK), 1)
    band = jnp.abs(ki - B_SIDE - qi) <= B_SIDE
    c0 = pl.program_id(2) * chunk

    def block(i, carry):
        m0 = pl.multiple_of(i * B_TQ, B_TQ)
        kpos = c0 + m0 - B_SIDE + ki
        valid = band & (kpos >= 0) & (kpos < n_rows)
        for pair in range(2):
            cols = slice(pair * LANES, (pair + 1) * LANES)
            q2 = q_ref[0, pl.ds(m0, B_TQ), cols]
            k2 = kw_ref[pl.ds(m0, B_TK), cols]
            v2 = vw_ref[pl.ds(m0, B_TK), cols]
            zero = jnp.zeros_like(q2)
            o_h, lse_h = [], []
            for half in range(2):
                qm = jnp.where(low, q2, zero) if half == 0 else jnp.where(low, zero, q2)
                s = jnp.where(valid, _dot_nt(qm, k2), NEG_INF)
                m = jnp.max(s, axis=-1, keepdims=True)
                p = jnp.exp(s - m)
                l = jnp.sum(p, axis=-1, keepdims=True)
                o_h.append(_dot(p.astype(BF16), v2) / l)
                lse_h.append(jnp.broadcast_to(m + jnp.log(l), (B_TQ, LANES)))
            o_ref[0, pl.ds(m0, B_TQ), cols] = jnp.where(low, o_h[0], o_h[1])
            lse_ref[0, pl.ds(m0, B_TQ), cols] = jnp.where(low, lse_h[0], lse_h[1])
        return carry

    lax.fori_loop(0, chunk // B_TQ, block, 0)


def _attn_b_group(qb, kb, vb, group, dil):
    bsz, s_len, _ = qb.shape
    n_rows = s_len // dil
    chunk = min(n_rows, B_CHUNK)
    n_col = B_W // B_OUT_W
    halo_per_chunk = chunk // B_SIDE
    last_halo = n_rows // B_SIDE - 1

    def view_in(x):
        return x.reshape(bsz, n_rows, dil * B_W)

    def col(r):
        return n_col * r + group

    cur = pl.BlockSpec((1, chunk, B_OUT_W), lambda b, r, c: (b, c, col(r)))
    prev = pl.BlockSpec((1, B_SIDE, B_OUT_W),
                        lambda b, r, c: (b, jnp.maximum(c * halo_per_chunk - 1, 0), col(r)))
    nxt = pl.BlockSpec((1, B_SIDE, B_OUT_W),
                       lambda b, r, c: (b, jnp.minimum((c + 1) * halo_per_chunk, last_halo), col(r)))
    out_spec = pl.BlockSpec((1, chunk, B_OUT_W), lambda b, r, c: (b, c, r))
    out_sds = jax.ShapeDtypeStruct((bsz, n_rows, dil * B_OUT_W), F32)
    window = pltpu.VMEM((chunk + 2 * B_SIDE, B_OUT_W), BF16)
    q, k, v = view_in(qb), view_in(kb), view_in(vb)
    o, lse = pl.pallas_call(
        functools.partial(_attn_b_kernel, n_rows=n_rows, chunk=chunk),
        grid=(bsz, dil, n_rows // chunk),
        in_specs=[cur, prev, cur, nxt, prev, cur, nxt],
        out_specs=[out_spec, out_spec],
        out_shape=[out_sds, out_sds],
        scratch_shapes=[window, window],
        compiler_params=_params(3),
        name=f"attn_b{group}",
    )(q, k, k, k, v, v, v)
    return o.reshape(bsz, s_len, B_OUT_W), lse.reshape(bsz, s_len, B_OUT_W)


def _mix_kernel(x_ref, nm_ref, wg_ref, oa_ref, ob0_ref, ob1_ref, ob2_ref,
                l0_ref, l1_ref, l2_ref, wba_ref, wbb_ref, wo_ref, y_ref):
    x = x_ref[0]
    h = _rms(x, nm_ref[...]).astype(BF16)
    l0, l1, l2 = l0_ref[0], l1_ref[0], l2_ref[0]
    m = jnp.maximum(jnp.maximum(l0, l1), l2)
    w0, w1, w2 = jnp.exp(l0 - m), jnp.exp(l1 - m), jnp.exp(l2 - m)
    comb = (w0 * ob0_ref[0] + w1 * ob1_ref[0] + w2 * ob2_ref[0]) / (w0 + w1 + w2)
    out_a = _dot(oa_ref[0], wba_ref[...])
    out_b = _dot(comb.astype(BF16), wbb_ref[...])
    ga = _dot(h, wg_ref[:, :D_MODEL])
    gb = _dot(h, wg_ref[:, D_MODEL:])
    merged = jax.nn.sigmoid(ga) * out_a + jax.nn.sigmoid(gb) * out_b
    y_ref[0] = x + _dot(merged.astype(BF16), wo_ref[...])


def _mix(x, nm, wg, oa, obs, lses, wba, wbb, wo):
    bsz, s_len, _ = x.shape
    tm = MIX_TM

    def tok(w):
        return pl.BlockSpec((1, tm, w), lambda b, j: (b, j, 0))

    return pl.pallas_call(
        _mix_kernel,
        grid=(bsz, s_len // tm),
        in_specs=[tok(D_MODEL), _resident((1, D_MODEL)), _resident((D_MODEL, 2 * D_MODEL)),
                  tok(A_Q_W)] + [tok(B_OUT_W)] * 6 +
                 [_resident((A_Q_W, D_MODEL)), _resident((B_OUT_W, D_MODEL)),
                  _resident((D_MODEL, D_MODEL))],
        out_specs=tok(D_MODEL),
        out_shape=jax.ShapeDtypeStruct((bsz, s_len, D_MODEL), F32),
        compiler_params=_params(2),
        name="mix",
    )(x, nm, wg, oa, *obs, *lses, wba, wbb, wo)


def _rope_tables(s_len):
    def cos_sin(pos, r, theta):
        inv = theta ** (-jnp.arange(0, r, 2, dtype=F32) / r)
        ang = pos[:, None] * inv[None, :]
        return jnp.cos(ang), jnp.sin(ang)

    t = jnp.arange(s_len, dtype=F32)
    rows = s_len // GRID_W
    row = jnp.repeat(jnp.arange(rows, dtype=F32), GRID_W)
    col = jnp.tile(jnp.arange(GRID_W, dtype=F32), rows)
    half = HEAD_DIM // 2
    cr, sr = cos_sin(row, half, AXIAL_THETA)
    cc, sc = cos_sin(col, half, AXIAL_THETA)
    cos_a = jnp.concatenate([cr, cr, cc, cc], axis=-1)
    sin_a = jnp.concatenate([-sr, sr, -sc, sc], axis=-1)
    cb, sb = cos_sin(t, ROT_DIMS, ROPE_THETA)
    rest = HEAD_DIM - ROT_DIMS
    cos_b = jnp.concatenate([cb, cb, jnp.ones((s_len, rest), F32)], axis=-1)
    sin_b = jnp.concatenate([-sb, sb, jnp.zeros((s_len, rest), F32)], axis=-1)
    return tuple(jnp.tile(x, (1, LANES // HEAD_DIM)) for x in (cos_a, sin_a, cos_b, sin_b))


def _layer(x, p):
    bsz, s_len, _ = x.shape
    x1 = _ffn(x.reshape(bsz * s_len, D_MODEL), p["n1"], p["w13_1"], p["w2_1"])
    x1 = x1.reshape(bsz, s_len, D_MODEL)
    qa, kt, va, qb, kb, vb = _proj(x1, p["nm"], p["wqkv"], p["qn"], p["kn"],
                                   _rope_tables(s_len))
    oa = _attn_a(qa, kt, va)
    obs, lses = [], []
    for g, (_, dil) in enumerate(B_WIN_DIL):
        o, lse = _attn_b_group(qb, kb, vb, g, dil)
        obs.append(o)
        lses.append(lse)
    x2 = _mix(x1, p["nm"], p["wg"], oa, obs, lses, p["wba"], p["wbb"], p["wo"])
    return x2.reshape(bsz * s_len, D_MODEL)


def kernel(x_prompt, x_sample, norm_ffn1, w13_ffn1, w2_ffn1, norm_mix, w_in, q_norm_a,
           k_norm_a, w_branch_a, w_branch_b, w_out, norm_ffn2, w13_ffn2, w2_ffn2, norm_final):
    depth = norm_ffn1.shape[0]
    layers = []
    for l in range(depth):
        layers.append(dict(
            n1=norm_ffn1[l][None, :], w13_1=w13_ffn1[l].astype(BF16), w2_1=w2_ffn1[l].astype(BF16),
            nm=norm_mix[l][None, :],
            wqkv=w_in[l, :, :QKV_W].astype(BF16), wg=w_in[l, :, QKV_W:].astype(BF16),
            qn=jnp.tile(q_norm_a[l], LANES // HEAD_DIM)[None, :],
            kn=jnp.tile(k_norm_a[l], LANES // HEAD_DIM)[None, :],
            wba=w_branch_a[l].astype(BF16), wbb=w_branch_b[l].astype(BF16),
            wo=w_out[l].astype(BF16),
            n2=norm_ffn2[l][None, :], w13_2=w13_ffn2[l].astype(BF16), w2_2=w2_ffn2[l].astype(BF16)))
    gf = norm_final[None, :]

    def trunk(x):
        bsz, s_len, _ = x.shape
        n_layers = len(layers)
        for l, p in enumerate(layers):
            x2 = _layer(x, p)
            last = l == n_layers - 1
            x = _ffn(x2, p["n2"], p["w13_2"], p["w2_2"], gf if last else None)
            x = x.reshape(bsz, s_len, D_MODEL)
        return x

    return (trunk(x_prompt), trunk(x_sample))
```

```python
import functools

import jax
import jax.numpy as jnp
from jax import lax
from jax.experimental import pallas as pl
from jax.experimental.pallas import tpu as pltpu

D_MODEL = 1024
HEAD_DIM = 64
A_Q_HEADS = 8
A_KV_HEADS = 2
B_WIN_DIL = ((128, 1), (512, 4), (2048, 16))
B_N_GROUPS = 3
B_HEADS = 12
D_FF = 2816
GRID_W = 64
ROPE_THETA = 500000.0
AXIAL_THETA = 10000.0
ROT_DIMS = HEAD_DIM // 4
EPS = 1e-6
NEG_INF = -1e30
A_Q_W = A_Q_HEADS * HEAD_DIM
A_KV_W = A_KV_HEADS * HEAD_DIM
B_W = B_HEADS * HEAD_DIM
B_OUT_W = 4 * HEAD_DIM
QKV_W = A_Q_W + 2 * A_KV_W + 3 * B_W
B_SIDE = 64
LOG2E = 1.4426950408889634
SAFE_LOG2 = 100.0

LANES = 128
VMEM_LIMIT = 56 * 1024 * 1024

BF16 = jnp.bfloat16
F32 = jnp.float32

FFN_TM = 512
FFN_TF = 256
PROJ_TM = 512
MIX_TM = 512
B_TQ = 128
B_TK = B_TQ + 2 * B_SIDE
B_CHUNK = 1024


def _dot(a, b):
    return jnp.dot(a, b, preferred_element_type=F32)


def _dot_nt(a, b):
    return lax.dot_general(a, b, (((1,), (1,)), ((), ())), preferred_element_type=F32)


def _rms(x, g):
    ms = jnp.mean(x * x, axis=-1, keepdims=True)
    return x * lax.rsqrt(ms + EPS) * g


def _resident(shape):
    nd = len(shape)
    return pl.BlockSpec(shape, lambda *_: (0,) * nd, pipeline_mode=pl.Buffered(1))


def _params(n_axes):
    return pltpu.CompilerParams(
        dimension_semantics=("arbitrary",) * n_axes, vmem_limit_bytes=VMEM_LIMIT)


def _ffn_kernel(*refs, final_norm):
    if final_norm:
        x_ref, g_ref, w13_ref, w2_ref, gf_ref, o_ref = refs
    else:
        x_ref, g_ref, w13_ref, w2_ref, o_ref = refs
    x = x_ref[...]
    h = _rms(x, g_ref[...]).astype(BF16)
    y = None
    for c in range(D_FF // FFN_TF):
        lo, hi = c * FFN_TF, (c + 1) * FFN_TF
        a = _dot(h, w13_ref[:, lo:hi])
        b = _dot(h, w13_ref[:, D_FF + lo:D_FF + hi])
        t = (a * jax.nn.sigmoid(a) * b).astype(BF16)
        d = _dot(t, w2_ref[lo:hi, :])
        y = d if y is None else y + d
    out = x + 0.5 * y
    if final_norm:
        out = _rms(out, gf_ref[...])
    o_ref[...] = out


def _ffn(x2d, g, w13, w2, gf=None):
    m = x2d.shape[0]
    final_norm = gf is not None
    row = pl.BlockSpec((FFN_TM, D_MODEL), lambda i: (i, 0))
    in_specs = [row, _resident((1, D_MODEL)), _resident((D_MODEL, 2 * D_FF)),
                _resident((D_FF, D_MODEL))]
    args = [x2d, g, w13, w2]
    if final_norm:
        in_specs.append(_resident((1, D_MODEL)))
        args.append(gf)
    return pl.pallas_call(
        functools.partial(_ffn_kernel, final_norm=final_norm),
        grid=(m // FFN_TM,),
        in_specs=in_specs,
        out_specs=row,
        out_shape=jax.ShapeDtypeStruct((m, D_MODEL), F32),
        compiler_params=_params(1),
        name="ffn_final" if final_norm else "ffn",
    )(*args)


def _swap_halves(y, half):
    lane = lax.broadcasted_iota(jnp.int32, y.shape, 1)
    up = pltpu.roll(y, LANES - half, 1)
    dn = pltpu.roll(y, half, 1)
    return jnp.where((lane & half) == 0, up, dn)


def _rope(y, cos, sin, half):
    return y * cos + _swap_halves(y, half) * sin


def _proj_kernel(x_ref, nm_ref, w_ref, qn_ref, kn_ref, ca_ref, sa_ref, cb_ref, sb_ref,
                 qa_ref, kt_ref, va_ref, qb_ref, kb_ref, vb_ref):
    h = _rms(x_ref[0], nm_ref[...]).astype(BF16)
    ca, sa, cb, sb = ca_ref[...], sa_ref[...], cb_ref[...], sb_ref[...]

    r = lax.broadcasted_iota(jnp.int32, (LANES, LANES), 0) // HEAD_DIM
    c = lax.broadcasted_iota(jnp.int32, (LANES, LANES), 1) // HEAD_DIM
    head_ones = jnp.where(r == c, 1.0, 0.0).astype(BF16)

    def head_norm_rope(xp, gain):
        sq = xp * xp
        sq_hi = sq.astype(BF16)
        sq_lo = (sq - sq_hi.astype(F32)).astype(BF16)
        ss = _dot(sq_hi, head_ones) + _dot(sq_lo, head_ones)
        y = xp * lax.rsqrt(ss * (1.0 / HEAD_DIM) + EPS) * gain
        return _rope(y, ca, sa, HEAD_DIM // 4)

    scale = HEAD_DIM ** -0.5
    for j in range(A_Q_W // LANES):
        xp = _dot(h, w_ref[:, j * LANES:(j + 1) * LANES])
        qa_ref[0, :, j * LANES:(j + 1) * LANES] = (
            head_norm_rope(xp, qn_ref[...]) * (scale * LOG2E)).astype(BF16)

    off = A_Q_W
    k = head_norm_rope(_dot(h, w_ref[:, off:off + A_KV_W]), kn_ref[...])
    kt = k.T.astype(BF16)
    for kv in range(A_KV_HEADS):
        one = kt[kv * HEAD_DIM:(kv + 1) * HEAD_DIM]
        kt_ref[0, kv] = jnp.concatenate([one, one], axis=0)
    off += A_KV_W
    va = _dot(h, w_ref[:, off:off + A_KV_W])
    va_ref[0] = jnp.concatenate([va, jnp.ones_like(va)], axis=1).astype(BF16)
    off += A_KV_W

    for j in range(B_W // LANES):
        xp = _dot(h, w_ref[:, off + j * LANES:off + (j + 1) * LANES])
        qb_ref[0, :, j * LANES:(j + 1) * LANES] = (
            _rope(xp, cb, sb, ROT_DIMS // 2) * scale).astype(BF16)
    off += B_W
    for j in range(B_W // LANES):
        xp = _dot(h, w_ref[:, off + j * LANES:off + (j + 1) * LANES])
        kb_ref[0, :, j * LANES:(j + 1) * LANES] = _rope(xp, cb, sb, ROT_DIMS // 2).astype(BF16)
    off += B_W
    vb_ref[0] = _dot(h, w_ref[:, off:off + B_W]).astype(BF16)


def _proj(x, nm, wqkv, qn, kn, tabs):
    bsz, s_len, _ = x.shape
    tm = PROJ_TM
    tab = pl.BlockSpec((tm, LANES), lambda b, j: (j, 0))

    def tok(w):
        return pl.BlockSpec((1, tm, w), lambda b, j: (b, j, 0))

    def out(w):
        return jax.ShapeDtypeStruct((bsz, s_len, w), BF16)

    return pl.pallas_call(
        _proj_kernel,
        grid=(bsz, s_len // tm),
        in_specs=[tok(D_MODEL), _resident((1, D_MODEL)), _resident((D_MODEL, QKV_W)),
                  _resident((1, LANES)), _resident((1, LANES)), tab, tab, tab, tab],
        out_specs=[tok(A_Q_W),
                   pl.BlockSpec((1, A_KV_HEADS, LANES, tm), lambda b, j: (b, 0, 0, j)),
                   tok(2 * LANES), tok(B_W), tok(B_W), tok(B_W)],
        out_shape=[out(A_Q_W),
                   jax.ShapeDtypeStruct((bsz, A_KV_HEADS, LANES, s_len), BF16),
                   out(2 * LANES), out(B_W), out(B_W), out(B_W)],
        compiler_params=_params(2),
        name="proj",
    )(x, nm, wqkv, qn, kn, *tabs)


def _attn_a_kernel(q_ref, kt_ref, v_ref, o_ref, *, tq):
    lane = lax.broadcasted_iota(jnp.int32, (tq, LANES), 1)
    low = lane < HEAD_DIM
    v = v_ref[0]

    qn2 = None
    for j in range(A_Q_W // LANES):
        qf = q_ref[0, :, j * LANES:(j + 1) * LANES].astype(F32)
        n2 = jnp.max(jnp.sum(qf * qf, axis=-1, keepdims=True))
        qn2 = n2 if qn2 is None else jnp.maximum(qn2, n2)
    kn2 = None
    for kv in range(A_KV_HEADS):
        kf = kt_ref[0, kv, 0:HEAD_DIM, :].astype(F32)
        n2 = jnp.max(jnp.sum(kf * kf, axis=0, keepdims=True))
        kn2 = n2 if kn2 is None else jnp.maximum(kn2, n2)
    safe = qn2 * kn2 <= SAFE_LOG2 * SAFE_LOG2

    def attend(shift_by_max):
        outs = []
        for kv in range(A_KV_HEADS):
            k2 = kt_ref[0, kv]
            rows = []
            for pair in range(2):
                c0 = kv * 4 * HEAD_DIM + pair * LANES
                q2 = q_ref[0, :, c0:c0 + LANES]
                zero = jnp.zeros_like(q2)
                rows.append(jnp.where(low, q2, zero))
                rows.append(jnp.where(low, zero, q2))
            s = _dot(jnp.concatenate(rows, axis=0), k2)
            if shift_by_max:
                s = s - jnp.max(s, axis=-1, keepdims=True)
            ol = _dot(jnp.exp2(s).astype(BF16), v)
            o = ol[:, :LANES] / ol[:, LANES:]
            for pair in range(2):
                a = o[(2 * pair) * tq:(2 * pair + 1) * tq]
                b = o[(2 * pair + 1) * tq:(2 * pair + 2) * tq]
                if kv == 0:
                    b = pltpu.roll(b, HEAD_DIM, 1)
                else:
                    a = pltpu.roll(a, HEAD_DIM, 1)
                outs.append(jnp.where(low, a, b))
        o_ref[0] = jnp.concatenate(outs, axis=1).astype(BF16)

    @pl.when(safe)
    def _():
        attend(shift_by_max=False)

    @pl.when(jnp.logical_not(safe))
    def _():
        attend(shift_by_max=True)


def _attn_a(qa, kt, va):
    bsz, s_len, _ = qa.shape
    tq = 128 if s_len <= 4096 else 64
    return pl.pallas_call(
        functools.partial(_attn_a_kernel, tq=tq),
        grid=(bsz, s_len // tq),
        in_specs=[pl.BlockSpec((1, tq, A_Q_W), lambda b, i: (b, i, 0)),
                  pl.BlockSpec((1, A_KV_HEADS, LANES, s_len), lambda b, i: (b, 0, 0, 0)),
                  pl.BlockSpec((1, s_len, 2 * LANES), lambda b, i: (b, 0, 0))],
        out_specs=pl.BlockSpec((1, tq, A_Q_W), lambda b, i: (b, i, 0)),
        out_shape=jax.ShapeDtypeStruct((bsz, s_len, A_Q_W), BF16),
        compiler_params=_params(2),
        name="attn_a",
    )(qa, kt, va)


def _attn_b_kernel(q_ref, kp_ref, kc_ref, kn_ref, vp_ref, vc_ref, vn_ref, o_ref, lse_ref,
                   kw_ref, vw_ref, *, n_rows, chunk):
    for w_ref, p_ref, c_ref, n_ref in ((kw_ref, kp_ref, kc_ref, kn_ref),
                                       (vw_ref, vp_ref, vc_ref, vn_ref)):
        w_ref[0:B_SIDE] = p_ref[0]
        w_ref[B_SIDE:B_SIDE + chunk] = c_ref[0]
        w_ref[B_SIDE + chunk:] = n_ref[0]

    lane = lax.broadcasted_iota(jnp.int32, (B_TQ, LANES), 1)
    low = lane < HEAD_DIM
    qi = lax.broadcasted_iota(jnp.int32, (B_TQ, B_TK), 0)
    ki = lax.broadcasted_iota(jnp.int32, (B_TQ, B_TK), 1)
    band = jnp.abs(ki - B_SIDE - qi) <= B_SIDE
    c0 = pl.program_id(2) * chunk

    def block(i, carry):
        m0 = pl.multiple_of(i * B_TQ, B_TQ)
        kpos = c0 + m0 - B_SIDE + ki
        valid = band & (kpos >= 0) & (kpos < n_rows)
        for pair in range(2):
            cols = slice(pair * LANES, (pair + 1) * LANES)
            q2 = q_ref[0, pl.ds(m0, B_TQ), cols]
            k2 = kw_ref[pl.ds(m0, B_TK), cols]
            v2 = vw_ref[pl.ds(m0, B_TK), cols]
            zero = jnp.zeros_like(q2)
            o_h, lse_h = [], []
            for half in range(2):
                qm = jnp.where(low, q2, zero) if half == 0 else jnp.where(low, zero, q2)
                s = jnp.where(valid, _dot_nt(qm, k2), NEG_INF)
                m = jnp.max(s, axis=-1, keepdims=True)
                p = jnp.exp(s - m)
                l = jnp.sum(p, axis=-1, keepdims=True)
                o_h.append(_dot(p.astype(BF16), v2) / l)
                lse_h.append(jnp.broadcast_to(m + jnp.log(l), (B_TQ, LANES)))
            o_ref[0, pl.ds(m0, B_TQ), cols] = jnp.where(low, o_h[0], o_h[1])
            lse_ref[0, pl.ds(m0, B_TQ), cols] = jnp.where(low, lse_h[0], lse_h[1])
        return carry

    lax.fori_loop(0, chunk // B_TQ, block, 0)


def _attn_b_group(qb, kb, vb, group, dil):
    bsz, s_len, _ = qb.shape
    n_rows = s_len // dil
    chunk = min(n_rows, B_CHUNK)
    n_col = B_W // B_OUT_W
    halo_per_chunk = chunk // B_SIDE
    last_halo = n_rows // B_SIDE - 1

    def view_in(x):
        return x.reshape(bsz, n_rows, dil * B_W)

    def col(r):
        return n_col * r + group

    cur = pl.BlockSpec((1, chunk, B_OUT_W), lambda b, r, c: (b, c, col(r)))
    prev = pl.BlockSpec((1, B_SIDE, B_OUT_W),
                        lambda b, r, c: (b, jnp.maximum(c * halo_per_chunk - 1, 0), col(r)))
    nxt = pl.BlockSpec((1, B_SIDE, B_OUT_W),
                       lambda b, r, c: (b, jnp.minimum((c + 1) * halo_per_chunk, last_halo), col(r)))
    out_spec = pl.BlockSpec((1, chunk, B_OUT_W), lambda b, r, c: (b, c, r))
    out_sds = jax.ShapeDtypeStruct((bsz, n_rows, dil * B_OUT_W), F32)
    window = pltpu.VMEM((chunk + 2 * B_SIDE, B_OUT_W), BF16)
    q, k, v = view_in(qb), view_in(kb), view_in(vb)
    o, lse = pl.pallas_call(
        functools.partial(_attn_b_kernel, n_rows=n_rows, chunk=chunk),
        grid=(bsz, dil, n_rows // chunk),
        in_specs=[cur, prev, cur, nxt, prev, cur, nxt],
        out_specs=[out_spec, out_spec],
        out_shape=[out_sds, out_sds],
        scratch_shapes=[window, window],
        compiler_params=_params(3),
        name=f"attn_b{group}",
    )(q, k, k, k, v, v, v)
    return o.reshape(bsz, s_len, B_OUT_W), lse.reshape(bsz, s_len, B_OUT_W)


def _mix_kernel(x_ref, nm_ref, wg_ref, oa_ref, ob0_ref, ob1_ref, ob2_ref,
                l0_ref, l1_ref, l2_ref, wba_ref, wbb_ref, wo_ref, y_ref):
    x = x_ref[0]
    h = _rms(x, nm_ref[...]).astype(BF16)
    l0, l1, l2 = l0_ref[0], l1_ref[0], l2_ref[0]
    m = jnp.maximum(jnp.maximum(l0, l1), l2)
    w0, w1, w2 = jnp.exp(l0 - m), jnp.exp(l1 - m), jnp.exp(l2 - m)
    comb = (w0 * ob0_ref[0] + w1 * ob1_ref[0] + w2 * ob2_ref[0]) / (w0 + w1 + w2)
    out_a = _dot(oa_ref[0], wba_ref[...])
    out_b = _dot(comb.astype(BF16), wbb_ref[...])
    ga = _dot(h, wg_ref[:, :D_MODEL])
    gb = _dot(h, wg_ref[:, D_MODEL:])
    merged = jax.nn.sigmoid(ga) * out_a + jax.nn.sigmoid(gb) * out_b
    y_ref[0] = x + _dot(merged.astype(BF16), wo_ref[...])


def _mix(x, nm, wg, oa, obs, lses, wba, wbb, wo):
    bsz, s_len, _ = x.shape
    tm = MIX_TM

    def tok(w):
        return pl.BlockSpec((1, tm, w), lambda b, j: (b, j, 0))

    return pl.pallas_call(
        _mix_kernel,
        grid=(bsz, s_len // tm),
        in_specs=[tok(D_MODEL), _resident((1, D_MODEL)), _resident((D_MODEL, 2 * D_MODEL)),
                  tok(A_Q_W)] + [tok(B_OUT_W)] * 6 +
                 [_resident((A_Q_W, D_MODEL)), _resident((B_OUT_W, D_MODEL)),
                  _resident((D_MODEL, D_MODEL))],
        out_specs=tok(D_MODEL),
        out_shape=jax.ShapeDtypeStruct((bsz, s_len, D_MODEL), F32),
        compiler_params=_params(2),
        name="mix",
    )(x, nm, wg, oa, *obs, *lses, wba, wbb, wo)


def _rope_tables(s_len):
    def cos_sin(pos, r, theta):
        inv = theta ** (-jnp.arange(0, r, 2, dtype=F32) / r)
        ang = pos[:, None] * inv[None, :]
        return jnp.cos(ang), jnp.sin(ang)

    t = jnp.arange(s_len, dtype=F32)
    rows = s_len // GRID_W
    row = jnp.repeat(jnp.arange(rows, dtype=F32), GRID_W)
    col = jnp.tile(jnp.arange(GRID_W, dtype=F32), rows)
    half = HEAD_DIM // 2
    cr, sr = cos_sin(row, half, AXIAL_THETA)
    cc, sc = cos_sin(col, half, AXIAL_THETA)
    cos_a = jnp.concatenate([cr, cr, cc, cc], axis=-1)
    sin_a = jnp.concatenate([-sr, sr, -sc, sc], axis=-1)
    cb, sb = cos_sin(t, ROT_DIMS, ROPE_THETA)
    rest = HEAD_DIM - ROT_DIMS
    cos_b = jnp.concatenate([cb, cb, jnp.ones((s_len, rest), F32)], axis=-1)
    sin_b = jnp.concatenate([-sb, sb, jnp.zeros((s_len, rest), F32)], axis=-1)
    return tuple(jnp.tile(x, (1, LANES // HEAD_DIM)) for x in (cos_a, sin_a, cos_b, sin_b))


def _layer(x, p):
    bsz, s_len, _ = x.shape
    x1 = _ffn(x.reshape(bsz * s_len, D_MODEL), p["n1"], p["w13_1"], p["w2_1"])
    x1 = x1.reshape(bsz, s_len, D_MODEL)
    qa, kt, va, qb, kb, vb = _proj(x1, p["nm"], p["wqkv"], p["qn"], p["kn"],
                                   _rope_tables(s_len))
    oa = _attn_a(qa, kt, va)
    obs, lses = [], []
    for g, (_, dil) in enumerate(B_WIN_DIL):
        o, lse = _attn_b_group(qb, kb, vb, g, dil)
        obs.append(o)
        lses.append(lse)
    x2 = _mix(x1, p["nm"], p["wg"], oa, obs, lses, p["wba"], p["wbb"], p["wo"])
    return x2.reshape(bsz * s_len, D_MODEL)


def kernel(x_prompt, x_sample, norm_ffn1, w13_ffn1, w2_ffn1, norm_mix, w_in, q_norm_a,
           k_norm_a, w_branch_a, w_branch_b, w_out, norm_ffn2, w13_ffn2, w2_ffn2, norm_final):
    depth = norm_ffn1.shape[0]
    layers = []
    for l in range(depth):
        layers.append(dict(
            n1=norm_ffn1[l][None, :], w13_1=w13_ffn1[l].astype(BF16), w2_1=w2_ffn1[l].astype(BF16),
            nm=norm_mix[l][None, :],
            wqkv=w_in[l, :, :QKV_W].astype(BF16), wg=w_in[l, :, QKV_W:].astype(BF16),
            qn=jnp.tile(q_norm_a[l], LANES // HEAD_DIM)[None, :],
            kn=jnp.tile(k_norm_a[l], LANES // HEAD_DIM)[None, :],
            wba=w_branch_a[l].astype(BF16), wbb=w_branch_b[l].astype(BF16),
            wo=w_out[l].astype(BF16),
            n2=norm_ffn2[l][None, :], w13_2=w13_ffn2[l].astype(BF16), w2_2=w2_ffn2[l].astype(BF16)))
    gf = norm_final[None, :]

    def trunk(x):
        bsz, s_len, _ = x.shape
        n_layers = len(layers)
        for l, p in enumerate(layers):
            x2 = _layer(x, p)
            last = l == n_layers - 1
            x = _ffn(x2, p["n2"], p["w13_2"], p["w2_2"], gf if last else None)
            x = x.reshape(bsz, s_len, D_MODEL)
        return x

    return (trunk(x_prompt), trunk(x_sample))
```

```python
import functools

import jax
import jax.numpy as jnp
from jax import lax
from jax.experimental import pallas as pl
from jax.experimental.pallas import tpu as pltpu

D_MODEL = 1024
HEAD_DIM = 64
A_Q_HEADS = 8
A_KV_HEADS = 2
B_WIN_DIL = ((128, 1), (512, 4), (2048, 16))
B_HEADS = 12
D_FF = 2816
GRID_W = 64
ROPE_THETA = 500000.0
AXIAL_THETA = 10000.0
ROT_DIMS = HEAD_DIM // 4
EPS = 1e-6
NEG_INF = -1e30
A_Q_W = A_Q_HEADS * HEAD_DIM
A_KV_W = A_KV_HEADS * HEAD_DIM
B_W = B_HEADS * HEAD_DIM
B_OUT_W = 4 * HEAD_DIM
QKV_W = A_Q_W + 2 * A_KV_W + 3 * B_W
B_SIDE = 64
LOG2E = 1.4426950408889634
Q_SCALE = HEAD_DIM ** -0.5 * LOG2E
SAFE_LOG2 = 100.0

LANES = 128
VMEM_LIMIT = 56 * 1024 * 1024

BF16 = jnp.bfloat16
F32 = jnp.float32

FFN_TM = 512
FFN_TF = 256
PROJ_TM = 512
MIX_TM = 512
A_TQ = 128
B_TQ = 128
B_TK = B_TQ + 2 * B_SIDE
B_CHUNK = 1024


def _dot(a, b):
    return jnp.dot(a, b, preferred_element_type=F32)


def _dot_nt(a, b):
    return lax.dot_general(a, b, (((1,), (1,)), ((), ())), preferred_element_type=F32)


def _rms(x, g):
    ms = jnp.mean(x * x, axis=-1, keepdims=True)
    return x * lax.rsqrt(ms + EPS) * g


def _resident(shape):
    nd = len(shape)
    return pl.BlockSpec(shape, lambda *_: (0,) * nd, pipeline_mode=pl.Buffered(1))


def _params(n_axes):
    return pltpu.CompilerParams(
        dimension_semantics=("arbitrary",) * n_axes, vmem_limit_bytes=VMEM_LIMIT)


def _ffn_kernel(*refs, final_norm):
    if final_norm:
        x_ref, g_ref, w13_ref, w2_ref, gf_ref, o_ref = refs
    else:
        x_ref, g_ref, w13_ref, w2_ref, o_ref = refs
    x = x_ref[...]
    h = _rms(x, g_ref[...]).astype(BF16)
    y = None
    for c in range(D_FF // FFN_TF):
        lo, hi = c * FFN_TF, (c + 1) * FFN_TF
        a = _dot(h, w13_ref[:, lo:hi])
        b = _dot(h, w13_ref[:, D_FF + lo:D_FF + hi])
        t = (a * jax.nn.sigmoid(a) * b).astype(BF16)
        d = _dot(t, w2_ref[lo:hi, :])
        y = d if y is None else y + d
    out = x + 0.5 * y
    if final_norm:
        out = _rms(out, gf_ref[...])
    o_ref[...] = out


def _ffn(x2d, g, w13, w2, gf=None):
    m = x2d.shape[0]
    final_norm = gf is not None
    row = pl.BlockSpec((FFN_TM, D_MODEL), lambda i: (i, 0))
    in_specs = [row, _resident((1, D_MODEL)), _resident((D_MODEL, 2 * D_FF)),
                _resident((D_FF, D_MODEL))]
    args = [x2d, g, w13, w2]
    if final_norm:
        in_specs.append(_resident((1, D_MODEL)))
        args.append(gf)
    return pl.pallas_call(
        functools.partial(_ffn_kernel, final_norm=final_norm),
        grid=(m // FFN_TM,),
        in_specs=in_specs,
        out_specs=row,
        out_shape=jax.ShapeDtypeStruct((m, D_MODEL), F32),
        compiler_params=_params(1),
        name="ffn_final" if final_norm else "ffn",
    )(*args)


def _swap_halves(y, half):
    lane = lax.broadcasted_iota(jnp.int32, y.shape, 1)
    up = pltpu.roll(y, LANES - half, 1)
    dn = pltpu.roll(y, half, 1)
    return jnp.where((lane & half) == 0, up, dn)


def _rope(y, cos, sin, half):
    return y * cos + _swap_halves(y, half) * sin


def _proj_kernel(x_ref, nm_ref, w_ref, qn_ref, kn_ref, ca_ref, sa_ref, cb_ref, sb_ref,
                 qa_ref, kt_ref, va_ref, *rest):
    b_refs, scr_ref = rest[:-1], rest[-1]
    tm = x_ref.shape[1]
    h = _rms(x_ref[0], nm_ref[...]).astype(BF16)
    ca, sa, cb, sb = ca_ref[...], sa_ref[...], cb_ref[...], sb_ref[...]

    r = lax.broadcasted_iota(jnp.int32, (LANES, LANES), 0) // HEAD_DIM
    c = lax.broadcasted_iota(jnp.int32, (LANES, LANES), 1) // HEAD_DIM
    head_ones = jnp.where(r == c, 1.0, 0.0).astype(BF16)

    def head_norm_rope(xp, gain):
        sq = xp * xp
        sq_hi = sq.astype(BF16)
        sq_lo = (sq - sq_hi.astype(F32)).astype(BF16)
        ss = _dot(sq_hi, head_ones) + _dot(sq_lo, head_ones)
        y = xp * lax.rsqrt(ss * (1.0 / HEAD_DIM) + EPS) * gain
        return _rope(y, ca, sa, HEAD_DIM // 4)

    def lanes(x, j):
        return x[:, j * LANES:(j + 1) * LANES]

    qa = _dot(h, w_ref[:, 0:A_Q_W])
    for j in range(A_Q_W // LANES):
        qa_ref[0, :, j * LANES:(j + 1) * LANES] = (
            head_norm_rope(lanes(qa, j), qn_ref[...]) * Q_SCALE).astype(BF16)

    off = A_Q_W
    kva = _dot(h, w_ref[:, off:off + 2 * A_KV_W])
    kt = head_norm_rope(lanes(kva, 0), kn_ref[...]).T.astype(BF16)
    for kv in range(A_KV_HEADS):
        one = kt[kv * HEAD_DIM:(kv + 1) * HEAD_DIM]
        kt_ref[0, kv] = jnp.concatenate([one, one], axis=0)
    va = lanes(kva, 1)
    va_ref[0] = jnp.concatenate([va, jnp.ones_like(va)], axis=1).astype(BF16)
    off += 2 * A_KV_W

    qkv_b = [_dot(h, w_ref[:, off + i * B_W:off + (i + 1) * B_W]) for i in range(3)]
    for g, (_, dil) in enumerate(B_WIN_DIL):
        for i, full in enumerate(qkv_b):
            out_ref = b_refs[3 * g + i]
            for s in range(B_OUT_W // LANES):
                blk = lanes(full, g * (B_OUT_W // LANES) + s)
                if i < 2:
                    blk = _rope(blk, cb, sb, ROT_DIMS // 2)
                if i == 0:
                    blk = blk * Q_SCALE
                cols = slice(s * LANES, (s + 1) * LANES)
                if dil == 1:
                    out_ref[0, 0, :, cols] = blk.astype(BF16)
                    continue
                slot = 3 * (g - 1) + i
                scr_ref[slot, s] = blk
                for res in range(dil):
                    rows = scr_ref[slot, s, pl.ds(res, tm // dil, stride=dil), :]
                    out_ref[0, res, :, cols] = rows.astype(BF16)


def _proj(x, nm, wqkv, qn, kn, tabs):
    bsz, s_len, _ = x.shape
    tm = PROJ_TM
    tab = pl.BlockSpec((tm, LANES), lambda b, j: (j, 0))

    def tok(w):
        return pl.BlockSpec((1, tm, w), lambda b, j: (b, j, 0))

    out_specs = [tok(A_Q_W),
                 pl.BlockSpec((1, A_KV_HEADS, LANES, tm), lambda b, j: (b, 0, 0, j)),
                 tok(2 * LANES)]
    out_shape = [jax.ShapeDtypeStruct((bsz, s_len, A_Q_W), BF16),
                 jax.ShapeDtypeStruct((bsz, A_KV_HEADS, LANES, s_len), BF16),
                 jax.ShapeDtypeStruct((bsz, s_len, 2 * LANES), BF16)]
    for _, dil in B_WIN_DIL:
        for _ in range(3):
            out_specs.append(pl.BlockSpec((1, dil, tm // dil, B_OUT_W),
                                          lambda b, j: (b, 0, j, 0)))
            out_shape.append(jax.ShapeDtypeStruct((bsz, dil, s_len // dil, B_OUT_W), BF16))
    n_strided = 3 * sum(1 for _, dil in B_WIN_DIL if dil > 1)
    return pl.pallas_call(
        _proj_kernel,
        grid=(bsz, s_len // tm),
        in_specs=[tok(D_MODEL), _resident((1, D_MODEL)), _resident((D_MODEL, QKV_W)),
                  _resident((1, LANES)), _resident((1, LANES)), tab, tab, tab, tab],
        out_specs=out_specs,
        out_shape=out_shape,
        scratch_shapes=[pltpu.VMEM((n_strided, B_OUT_W // LANES, tm, LANES), F32)],
        compiler_params=_params(2),
        name="proj",
    )(x, nm, wqkv, qn, kn, *tabs)


def _attn_a_kernel(qn_ref, kn_ref, q_ref, kt_ref, v_ref, o_ref, safe_ref):
    tq = q_ref.shape[1]
    lane = lax.broadcasted_iota(jnp.int32, (tq, LANES), 1)
    low = lane < HEAD_DIM

    @pl.when((pl.program_id(0) == 0) & (pl.program_id(1) == 0))
    def _():
        gq2 = jnp.max(qn_ref[...] * qn_ref[...])
        gk2 = jnp.max(kn_ref[...] * kn_ref[...])
        bound2 = (HEAD_DIM * Q_SCALE) ** 2 * 1.05 * gq2 * gk2
        safe_ref[0] = (bound2 <= SAFE_LOG2 * SAFE_LOG2).astype(jnp.int32)

    safe = safe_ref[0] == 1

    def head_rows(kv):
        rows = []
        for pair in range(2):
            c0 = kv * 4 * HEAD_DIM + pair * LANES
            q2 = q_ref[0, :, c0:c0 + LANES]
            zero = jnp.zeros_like(q2)
            rows.append(jnp.where(low, q2, zero))
            rows.append(jnp.where(low, zero, q2))
        return rows

    def place(kv, per_head):
        outs = []
        for pair in range(2):
            a, b = per_head[2 * pair], per_head[2 * pair + 1]
            if kv == 0:
                b = pltpu.roll(b, HEAD_DIM, 1)
            else:
                a = pltpu.roll(a, HEAD_DIM, 1)
            outs.append(jnp.where(low, a, b))
        return outs

    @pl.when(safe)
    def _():
        outs = []
        for kv in range(A_KV_HEADS):
            s = _dot(jnp.concatenate(head_rows(kv), axis=0), kt_ref[0, kv])
            ol = _dot(jnp.exp2(s).astype(BF16), v_ref[0])
            o = ol[:, :LANES] / ol[:, LANES:]
            outs += place(kv, [o[g * tq:(g + 1) * tq] for g in range(4)])
        o_ref[0] = jnp.concatenate(outs, axis=1).astype(BF16)

    @pl.when(jnp.logical_not(safe))
    def _():
        outs = []
        for kv in range(A_KV_HEADS):
            per_head = []
            for qm in head_rows(kv):
                s = _dot(qm, kt_ref[0, kv])
                s = s - jnp.max(s, axis=-1, keepdims=True)
                ol = _dot(jnp.exp2(s).astype(BF16), v_ref[0])
                per_head.append(ol[:, :LANES] / ol[:, LANES:])
            outs += place(kv, per_head)
        o_ref[0] = jnp.concatenate(outs, axis=1).astype(BF16)


def _attn_a(qn, kn, qa, kt, va):
    bsz, s_len, _ = qa.shape
    tq = A_TQ
    return pl.pallas_call(
        _attn_a_kernel,
        grid=(bsz, s_len // tq),
        in_specs=[_resident((1, LANES)), _resident((1, LANES)),
                  pl.BlockSpec((1, tq, A_Q_W), lambda b, i: (b, i, 0)),
                  pl.BlockSpec((1, A_KV_HEADS, LANES, s_len), lambda b, i: (b, 0, 0, 0)),
                  pl.BlockSpec((1, s_len, 2 * LANES), lambda b, i: (b, 0, 0))],
        out_specs=pl.BlockSpec((1, tq, A_Q_W), lambda b, i: (b, i, 0)),
        out_shape=jax.ShapeDtypeStruct((bsz, s_len, A_Q_W), BF16),
        scratch_shapes=[pltpu.SMEM((1,), jnp.int32)],
        compiler_params=_params(2),
        name="attn_a",
    )(qn, kn, qa, kt, va)


def _attn_b_kernel(q_ref, kp_ref, kc_ref, kn_ref, vp_ref, vc_ref, vn_ref, o_ref, lse_ref,
                   kw_ref, vw_ref, *, n_rows):
    n_cls, chunk = q_ref.shape[1], q_ref.shape[2]
    for w_ref, p_ref, c_ref, n_ref in ((kw_ref, kp_ref, kc_ref, kn_ref),
                                       (vw_ref, vp_ref, vc_ref, vn_ref)):
        w_ref[:, 0:B_SIDE] = p_ref[0]
        w_ref[:, B_SIDE:B_SIDE + chunk] = c_ref[0]
        w_ref[:, B_SIDE + chunk:] = n_ref[0]

    lane = lax.broadcasted_iota(jnp.int32, (B_TQ, LANES), 1)
    low = lane < HEAD_DIM
    qi = lax.broadcasted_iota(jnp.int32, (B_TQ, B_TK), 0)
    ki = lax.broadcasted_iota(jnp.int32, (B_TQ, B_TK), 1)
    band = jnp.abs(ki - B_SIDE - qi) <= B_SIDE
    c0 = pl.program_id(2) * chunk
    ones = jnp.ones((B_TK, LANES), BF16)

    n_blk = chunk // B_TQ
    for blk in range(n_blk):
        m0 = blk * B_TQ
        valid = band
        if blk in (0, n_blk - 1):
            kpos = c0 + (m0 - B_SIDE) + ki
            valid = band & (kpos >= 0) & (kpos < n_rows)
        for cls in range(n_cls):
            for pair in range(2):
                cols = slice(pair * LANES, (pair + 1) * LANES)
                q2 = q_ref[0, cls, m0:m0 + B_TQ, cols]
                k2 = kw_ref[cls, m0:m0 + B_TK, cols]
                v2 = jnp.concatenate([vw_ref[cls, m0:m0 + B_TK, cols], ones], axis=1)
                zero = jnp.zeros_like(q2)
                o_h, lse_h = [], []
                for half in range(2):
                    qm = jnp.where(low, q2, zero) if half == 0 else jnp.where(low, zero, q2)
                    s = jnp.where(valid, _dot_nt(qm, k2), NEG_INF)
                    m = jnp.max(s, axis=-1, keepdims=True)
                    ol = _dot(jnp.exp2(s - m).astype(BF16), v2)
                    l = ol[:, LANES:]
                    o_h.append(ol[:, :LANES] / l)
                    lse_h.append(m + jnp.log(l) * LOG2E)
                o_ref[0, cls, m0:m0 + B_TQ, cols] = jnp.where(low, o_h[0], o_h[1])
                lse_ref[0, cls, m0:m0 + B_TQ, cols] = jnp.where(low, lse_h[0], lse_h[1])


def _attn_b_group(q, k, v):
    bsz, dil, n_rows, _ = q.shape
    chunk = min(n_rows, B_CHUNK)
    n_cls = min(dil, B_CHUNK // chunk)
    halo_per_chunk = chunk // B_SIDE
    last_halo = n_rows // B_SIDE - 1

    cur = pl.BlockSpec((1, n_cls, chunk, B_OUT_W), lambda b, r, c: (b, r, c, 0))
    prev = pl.BlockSpec((1, n_cls, B_SIDE, B_OUT_W),
                        lambda b, r, c: (b, r, jnp.maximum(c * halo_per_chunk - 1, 0), 0))
    nxt = pl.BlockSpec((1, n_cls, B_SIDE, B_OUT_W),
                       lambda b, r, c: (b, r, jnp.minimum((c + 1) * halo_per_chunk, last_halo), 0))
    out_sds = jax.ShapeDtypeStruct((bsz, dil, n_rows, B_OUT_W), F32)
    window = pltpu.VMEM((n_cls, chunk + 2 * B_SIDE, B_OUT_W), BF16)
    return pl.pallas_call(
        functools.partial(_attn_b_kernel, n_rows=n_rows),
        grid=(bsz, dil // n_cls, n_rows // chunk),
        in_specs=[cur, prev, cur, nxt, prev, cur, nxt],
        out_specs=[cur, cur],
        out_shape=[out_sds, out_sds],
        scratch_shapes=[window, window],
        compiler_params=_params(3),
        name=f"attn_b_d{dil}",
    )(q, k, k, k, v, v, v)


def _mix_kernel(x_ref, nm_ref, wg_ref, oa_ref, ob0_ref, ob1_ref, ob2_ref,
                l0_ref, l1_ref, l2_ref, wba_ref, wbb_ref, wo_ref, y_ref, scr_ref):
    x = x_ref[0]
    tm = x.shape[0]
    h = _rms(x, nm_ref[...]).astype(BF16)

    def natural_order(slot, blk_ref):
        dil = blk_ref.shape[1]
        if dil == 1:
            return blk_ref[0, 0]
        for s in range(B_OUT_W // LANES):
            for res in range(dil):
                scr_ref[slot, s, pl.ds(res, tm // dil, stride=dil), :] = (
                    blk_ref[0, res, :, s * LANES:(s + 1) * LANES])
        return jnp.concatenate([scr_ref[slot, s] for s in range(B_OUT_W // LANES)], axis=1)

    obs = [natural_order(2 * g, ref) for g, ref in enumerate((ob0_ref, ob1_ref, ob2_ref))]
    l0, l1, l2 = [natural_order(2 * g + 1, ref) for g, ref in enumerate((l0_ref, l1_ref, l2_ref))]
    m = jnp.maximum(jnp.maximum(l0, l1), l2)
    w0, w1, w2 = jnp.exp2(l0 - m), jnp.exp2(l1 - m), jnp.exp2(l2 - m)
    comb = (w0 * obs[0] + w1 * obs[1] + w2 * obs[2]) / (w0 + w1 + w2)
    out_a = _dot(oa_ref[0], wba_ref[...])
    out_b = _dot(comb.astype(BF16), wbb_ref[...])
    ga = _dot(h, wg_ref[:, :D_MODEL])
    gb = _dot(h, wg_ref[:, D_MODEL:])
    merged = jax.nn.sigmoid(ga) * out_a + jax.nn.sigmoid(gb) * out_b
    y_ref[0] = x + _dot(merged.astype(BF16), wo_ref[...])


def _mix(x, nm, wg, oa, obs, lses, wba, wbb, wo):
    bsz, s_len, _ = x.shape
    tm = MIX_TM

    def tok(w):
        return pl.BlockSpec((1, tm, w), lambda b, j: (b, j, 0))

    def classes(a):
        dil = a.shape[1]
        return pl.BlockSpec((1, dil, tm // dil, B_OUT_W), lambda b, j: (b, 0, j, 0))

    return pl.pallas_call(
        _mix_kernel,
        grid=(bsz, s_len // tm),
        in_specs=[tok(D_MODEL), _resident((1, D_MODEL)), _resident((D_MODEL, 2 * D_MODEL)),
                  tok(A_Q_W)] + [classes(a) for a in obs] + [classes(a) for a in lses] +
                 [_resident((A_Q_W, D_MODEL)), _resident((B_OUT_W, D_MODEL)),
                  _resident((D_MODEL, D_MODEL))],
        out_specs=tok(D_MODEL),
        out_shape=jax.ShapeDtypeStruct((bsz, s_len, D_MODEL), F32),
        scratch_shapes=[pltpu.VMEM((2 * len(B_WIN_DIL), B_OUT_W // LANES, tm, LANES), F32)],
        compiler_params=_params(2),
        name="mix",
    )(x, nm, wg, oa, *obs, *lses, wba, wbb, wo)


def _rope_tables(s_len):
    def cos_sin(pos, r, theta):
        inv = theta ** (-jnp.arange(0, r, 2, dtype=F32) / r)
        ang = pos[:, None] * inv[None, :]
        return jnp.cos(ang), jnp.sin(ang)

    t = jnp.arange(s_len, dtype=F32)
    rows = s_len // GRID_W
    row = jnp.repeat(jnp.arange(rows, dtype=F32), GRID_W)
    col = jnp.tile(jnp.arange(GRID_W, dtype=F32), rows)
    half = HEAD_DIM // 2
    cr, sr = cos_sin(row, half, AXIAL_THETA)
    cc, sc = cos_sin(col, half, AXIAL_THETA)
    cos_a = jnp.concatenate([cr, cr, cc, cc], axis=-1)
    sin_a = jnp.concatenate([-sr, sr, -sc, sc], axis=-1)
    cb, sb = cos_sin(t, ROT_DIMS, ROPE_THETA)
    rest = HEAD_DIM - ROT_DIMS
    cos_b = jnp.concatenate([cb, cb, jnp.ones((s_len, rest), F32)], axis=-1)
    sin_b = jnp.concatenate([-sb, sb, jnp.zeros((s_len, rest), F32)], axis=-1)
    return tuple(jnp.tile(x, (1, LANES // HEAD_DIM)) for x in (cos_a, sin_a, cos_b, sin_b))


def _layer(x, p):
    bsz, s_len, _ = x.shape
    x1 = _ffn(x.reshape(bsz * s_len, D_MODEL), p["n1"], p["w13_1"], p["w2_1"])
    x1 = x1.reshape(bsz, s_len, D_MODEL)
    qa, kt, va, *qkv_b = _proj(x1, p["nm"], p["wqkv"], p["qn"], p["kn"], _rope_tables(s_len))
    oa = _attn_a(p["qn"], p["kn"], qa, kt, va)
    obs, lses = [], []
    for g in range(len(B_WIN_DIL)):
        o, lse = _attn_b_group(*qkv_b[3 * g:3 * g + 3])
        obs.append(o)
        lses.append(lse)
    x2 = _mix(x1, p["nm"], p["wg"], oa, obs, lses, p["wba"], p["wbb"], p["wo"])
    return x2.reshape(bsz * s_len, D_MODEL)


def kernel(x_prompt, x_sample, norm_ffn1, w13_ffn1, w2_ffn1, norm_mix, w_in, q_norm_a,
           k_norm_a, w_branch_a, w_branch_b, w_out, norm_ffn2, w13_ffn2, w2_ffn2, norm_final):
    depth = norm_ffn1.shape[0]
    layers = []
    for l in range(depth):
        layers.append(dict(
            n1=norm_ffn1[l][None, :], w13_1=w13_ffn1[l].astype(BF16), w2_1=w2_ffn1[l].astype(BF16),
            nm=norm_mix[l][None, :],
            wqkv=w_in[l, :, :QKV_W].astype(BF16), wg=w_in[l, :, QKV_W:].astype(BF16),
            qn=jnp.tile(q_norm_a[l], LANES // HEAD_DIM)[None, :],
            kn=jnp.tile(k_norm_a[l], LANES // HEAD_DIM)[None, :],
            wba=w_branch_a[l].astype(BF16), wbb=w_branch_b[l].astype(BF16),
            wo=w_out[l].astype(BF16),
            n2=norm_ffn2[l][None, :], w13_2=w13_ffn2[l].astype(BF16), w2_2=w2_ffn2[l].astype(BF16)))
    gf = norm_final[None, :]

    def trunk(x):
        bsz, s_len, _ = x.shape
        n_layers = len(layers)
        for l, p in enumerate(layers):
            x2 = _layer(x, p)
            last = l == n_layers - 1
            x = _ffn(x2, p["n2"], p["w13_2"], p["w2_2"], gf if last else None)
            x = x.reshape(bsz, s_len, D_MODEL)
        return x

    return (trunk(x_prompt), trunk(x_sample))
```

```python
import functools

import jax
import jax.numpy as jnp
from jax import lax
from jax.experimental import pallas as pl
from jax.experimental.pallas import tpu as pltpu

D_MODEL = 1024
HEAD_DIM = 64
A_Q_HEADS = 8
A_KV_HEADS = 2
B_WIN_DIL = ((128, 1), (512, 4), (2048, 16))
B_HEADS = 12
D_FF = 2816
GRID_W = 64
ROPE_THETA = 500000.0
AXIAL_THETA = 10000.0
ROT_DIMS = HEAD_DIM // 4
EPS = 1e-6
NEG_INF = -1e30
A_Q_W = A_Q_HEADS * HEAD_DIM
A_KV_W = A_KV_HEADS * HEAD_DIM
B_W = B_HEADS * HEAD_DIM
B_OUT_W = 4 * HEAD_DIM
QKV_W = A_Q_W + 2 * A_KV_W + 3 * B_W
B_SIDE = 64
LOG2E = 1.4426950408889634
Q_SCALE = HEAD_DIM ** -0.5 * LOG2E
SAFE_LOG2 = 100.0

LANES = 128
VMEM_LIMIT = 56 * 1024 * 1024

BF16 = jnp.bfloat16
F32 = jnp.float32

FFN_TM = 1024
FFN_TF = 256
PROJ_TM = 512
MIX_TM = 512
A_TQ = 256
B_TQ = 128
B_TK = B_TQ + 2 * B_SIDE
B_CHUNK = 1024


def _dot(a, b):
    return jnp.dot(a, b, preferred_element_type=F32)


def _dot_nt(a, b):
    return lax.dot_general(a, b, (((1,), (1,)), ((), ())), preferred_element_type=F32)


def _rms(x, g):
    ms = jnp.mean(x * x, axis=-1, keepdims=True)
    return x * lax.rsqrt(ms + EPS) * g


def _resident(shape):
    nd = len(shape)
    return pl.BlockSpec(shape, lambda *_: (0,) * nd, pipeline_mode=pl.Buffered(1))


def _params(n_axes):
    return pltpu.CompilerParams(
        dimension_semantics=("arbitrary",) * n_axes, vmem_limit_bytes=VMEM_LIMIT)


def _ffn_kernel(*refs, final_norm):
    if final_norm:
        x_ref, g_ref, w13_ref, w2_ref, gf_ref, o_ref = refs
    else:
        x_ref, g_ref, w13_ref, w2_ref, o_ref = refs
    x = x_ref[...]
    h = _rms(x, g_ref[...]).astype(BF16)
    y = None
    for c in range(D_FF // FFN_TF):
        lo, hi = c * FFN_TF, (c + 1) * FFN_TF
        a = _dot(h, w13_ref[:, lo:hi])
        b = _dot(h, w13_ref[:, D_FF + lo:D_FF + hi])
        t = (a * jax.nn.sigmoid(a) * b).astype(BF16)
        d = _dot(t, w2_ref[lo:hi, :])
        y = d if y is None else y + d
    out = x + 0.5 * y
    if final_norm:
        out = _rms(out, gf_ref[...])
    o_ref[...] = out


def _ffn(x2d, g, w13, w2, gf=None):
    m = x2d.shape[0]
    final_norm = gf is not None
    row = pl.BlockSpec((FFN_TM, D_MODEL), lambda i: (i, 0))
    in_specs = [row, _resident((1, D_MODEL)), _resident((D_MODEL, 2 * D_FF)),
                _resident((D_FF, D_MODEL))]
    args = [x2d, g, w13, w2]
    if final_norm:
        in_specs.append(_resident((1, D_MODEL)))
        args.append(gf)
    return pl.pallas_call(
        functools.partial(_ffn_kernel, final_norm=final_norm),
        grid=(m // FFN_TM,),
        in_specs=in_specs,
        out_specs=row,
        out_shape=jax.ShapeDtypeStruct((m, D_MODEL), F32),
        compiler_params=_params(1),
        name="ffn_final" if final_norm else "ffn",
    )(*args)


def _swap_halves(y, half):
    lane = lax.broadcasted_iota(jnp.int32, y.shape, 1)
    up = pltpu.roll(y, LANES - half, 1)
    dn = pltpu.roll(y, half, 1)
    return jnp.where((lane & half) == 0, up, dn)


def _rope(y, cos, sin, half):
    return y * cos + _swap_halves(y, half) * sin


def _proj_kernel(x_ref, nm_ref, w_ref, qn_ref, kn_ref, ca_ref, sa_ref, cb_ref, sb_ref,
                 qa_ref, kt_ref, va_ref, *rest):
    b_refs, scr_ref = rest[:-1], rest[-1]
    tm = x_ref.shape[1]
    h = _rms(x_ref[0], nm_ref[...]).astype(BF16)
    ca, sa, cb, sb = ca_ref[...], sa_ref[...], cb_ref[...], sb_ref[...]

    r = lax.broadcasted_iota(jnp.int32, (2 * LANES, LANES), 0) % LANES // HEAD_DIM
    c = lax.broadcasted_iota(jnp.int32, (2 * LANES, LANES), 1) // HEAD_DIM
    head_ones = jnp.where(r == c, 1.0, 0.0).astype(BF16)

    def head_norm_rope(xp, gain):
        sq = xp * xp
        sq_hi = sq.astype(BF16)
        sq_lo = (sq - sq_hi.astype(F32)).astype(BF16)
        ss = _dot(jnp.concatenate([sq_hi, sq_lo], axis=1), head_ones)
        y = xp * lax.rsqrt(ss * (1.0 / HEAD_DIM) + EPS) * gain
        return _rope(y, ca, sa, HEAD_DIM // 4)

    def lanes(x, j):
        return x[:, j * LANES:(j + 1) * LANES]

    off = A_Q_W + 2 * A_KV_W
    qkv_b = [_dot(h, w_ref[:, off + i * B_W:off + (i + 1) * B_W]) for i in range(3)]
    for g, (_, dil) in reversed(list(enumerate(B_WIN_DIL))):
        for i, full in enumerate(qkv_b):
            out_ref = b_refs[3 * g + i]
            for s in range(B_OUT_W // LANES):
                blk = lanes(full, g * (B_OUT_W // LANES) + s)
                if i < 2:
                    blk = _rope(blk, cb, sb, ROT_DIMS // 2)
                if i == 0:
                    blk = blk * Q_SCALE
                cols = slice(s * LANES, (s + 1) * LANES)
                if dil == 1:
                    out_ref[0, 0, :, cols] = blk.astype(BF16)
                    continue
                slot = 3 * (g - 1) + i
                scr_ref[slot, s] = blk
                for res in range(dil):
                    rows = scr_ref[slot, s, pl.ds(res, tm // dil, stride=dil), :]
                    out_ref[0, res, :, cols] = rows.astype(BF16)

    qa = _dot(h, w_ref[:, 0:A_Q_W])
    for j in range(A_Q_W // LANES):
        qa_ref[0, :, j * LANES:(j + 1) * LANES] = (
            head_norm_rope(lanes(qa, j), qn_ref[...]) * Q_SCALE).astype(BF16)

    off = A_Q_W
    kva = _dot(h, w_ref[:, off:off + 2 * A_KV_W])
    kt = head_norm_rope(lanes(kva, 0), kn_ref[...]).T.astype(BF16)
    for kv in range(A_KV_HEADS):
        one = kt[kv * HEAD_DIM:(kv + 1) * HEAD_DIM]
        kt_ref[0, kv] = jnp.concatenate([one, one], axis=0)
    va = lanes(kva, 1)
    va_ref[0] = jnp.concatenate([va, jnp.ones_like(va)], axis=1).astype(BF16)


def _proj(x, nm, wqkv, qn, kn, tabs):
    bsz, s_len, _ = x.shape
    tm = PROJ_TM
    tab = pl.BlockSpec((tm, LANES), lambda b, j: (j, 0))

    def tok(w):
        return pl.BlockSpec((1, tm, w), lambda b, j: (b, j, 0))

    out_specs = [tok(A_Q_W),
                 pl.BlockSpec((1, A_KV_HEADS, LANES, tm), lambda b, j: (b, 0, 0, j)),
                 tok(2 * LANES)]
    out_shape = [jax.ShapeDtypeStruct((bsz, s_len, A_Q_W), BF16),
                 jax.ShapeDtypeStruct((bsz, A_KV_HEADS, LANES, s_len), BF16),
                 jax.ShapeDtypeStruct((bsz, s_len, 2 * LANES), BF16)]
    for _, dil in B_WIN_DIL:
        for _ in range(3):
            out_specs.append(pl.BlockSpec((1, dil, tm // dil, B_OUT_W),
                                          lambda b, j: (b, 0, j, 0)))
            out_shape.append(jax.ShapeDtypeStruct((bsz, dil, s_len // dil, B_OUT_W), BF16))
    n_strided = 3 * sum(1 for _, dil in B_WIN_DIL if dil > 1)
    return pl.pallas_call(
        _proj_kernel,
        grid=(bsz, s_len // tm),
        in_specs=[tok(D_MODEL), _resident((1, D_MODEL)), _resident((D_MODEL, QKV_W)),
                  _resident((1, LANES)), _resident((1, LANES)), tab, tab, tab, tab],
        out_specs=out_specs,
        out_shape=out_shape,
        scratch_shapes=[pltpu.VMEM((n_strided, B_OUT_W // LANES, tm, LANES), F32)],
        compiler_params=_params(2),
        name="proj",
    )(x, nm, wqkv, qn, kn, *tabs)


def _attn_a_kernel(qn_ref, kn_ref, q_ref, kt_ref, v_ref, o_ref, safe_ref):
    tq = q_ref.shape[1]
    lane = lax.broadcasted_iota(jnp.int32, (tq, LANES), 1)
    low = lane < HEAD_DIM

    @pl.when((pl.program_id(0) == 0) & (pl.program_id(1) == 0))
    def _():
        gq2 = jnp.max(qn_ref[...] * qn_ref[...])
        gk2 = jnp.max(kn_ref[...] * kn_ref[...])
        bound2 = (HEAD_DIM * Q_SCALE) ** 2 * 1.05 * gq2 * gk2
        safe_ref[0] = (bound2 <= SAFE_LOG2 * SAFE_LOG2).astype(jnp.int32)

    safe = safe_ref[0] == 1

    def head_rows(kv):
        rows = []
        for pair in range(2):
            c0 = kv * 4 * HEAD_DIM + pair * LANES
            q2 = q_ref[0, :, c0:c0 + LANES]
            zero = jnp.zeros_like(q2)
            rows.append(jnp.where(low, q2, zero))
            rows.append(jnp.where(low, zero, q2))
        return rows

    def place(kv, per_head):
        outs = []
        for pair in range(2):
            a, b = per_head[2 * pair], per_head[2 * pair + 1]
            if kv == 0:
                b = pltpu.roll(b, HEAD_DIM, 1)
            else:
                a = pltpu.roll(a, HEAD_DIM, 1)
            outs.append(jnp.where(low, a, b))
        return outs

    @pl.when(safe)
    def _():
        outs = []
        for kv in range(A_KV_HEADS):
            s = _dot(jnp.concatenate(head_rows(kv), axis=0), kt_ref[0, kv])
            ol = _dot(jnp.exp2(s).astype(BF16), v_ref[0])
            o = ol[:, :LANES] / ol[:, LANES:]
            outs += place(kv, [o[g * tq:(g + 1) * tq] for g in range(4)])
        o_ref[0] = jnp.concatenate(outs, axis=1).astype(BF16)

    @pl.when(jnp.logical_not(safe))
    def _():
        outs = []
        for kv in range(A_KV_HEADS):
            per_head = []
            for qm in head_rows(kv):
                s = _dot(qm, kt_ref[0, kv])
                s = s - jnp.max(s, axis=-1, keepdims=True)
                ol = _dot(jnp.exp2(s).astype(BF16), v_ref[0])
                per_head.append(ol[:, :LANES] / ol[:, LANES:])
            outs += place(kv, per_head)
        o_ref[0] = jnp.concatenate(outs, axis=1).astype(BF16)


def _attn_a(qn, kn, qa, kt, va):
    bsz, s_len, _ = qa.shape
    tq = A_TQ
    return pl.pallas_call(
        _attn_a_kernel,
        grid=(bsz, s_len // tq),
        in_specs=[_resident((1, LANES)), _resident((1, LANES)),
                  pl.BlockSpec((1, tq, A_Q_W), lambda b, i: (b, i, 0)),
                  pl.BlockSpec((1, A_KV_HEADS, LANES, s_len), lambda b, i: (b, 0, 0, 0)),
                  pl.BlockSpec((1, s_len, 2 * LANES), lambda b, i: (b, 0, 0))],
        out_specs=pl.BlockSpec((1, tq, A_Q_W), lambda b, i: (b, i, 0)),
        out_shape=jax.ShapeDtypeStruct((bsz, s_len, A_Q_W), BF16),
        scratch_shapes=[pltpu.SMEM((1,), jnp.int32)],
        compiler_params=_params(2),
        name="attn_a",
    )(qn, kn, qa, kt, va)


def _attn_b_kernel(q_ref, kp_ref, kc_ref, kn_ref, vp_ref, vc_ref, vn_ref, o_ref, lse_ref,
                   kw_ref, vw_ref, *, n_rows):
    n_cls, chunk = q_ref.shape[1], q_ref.shape[2]
    for w_ref, p_ref, c_ref, n_ref in ((kw_ref, kp_ref, kc_ref, kn_ref),
                                       (vw_ref, vp_ref, vc_ref, vn_ref)):
        w_ref[:, 0:B_SIDE] = p_ref[0]
        w_ref[:, B_SIDE:B_SIDE + chunk] = c_ref[0]
        w_ref[:, B_SIDE + chunk:] = n_ref[0]

    lane = lax.broadcasted_iota(jnp.int32, (B_TQ, LANES), 1)
    low = lane < HEAD_DIM
    qi = lax.broadcasted_iota(jnp.int32, (B_TQ, B_TK), 0)
    ki = lax.broadcasted_iota(jnp.int32, (B_TQ, B_TK), 1)
    band = jnp.abs(ki - B_SIDE - qi) <= B_SIDE
    c0 = pl.program_id(2) * chunk
    ones = jnp.ones((B_TK, LANES), BF16)

    n_blk = chunk // B_TQ
    for blk in range(n_blk):
        m0 = blk * B_TQ
        valid = band
        if blk in (0, n_blk - 1):
            kpos = c0 + (m0 - B_SIDE) + ki
            valid = band & (kpos >= 0) & (kpos < n_rows)
        for cls in range(n_cls):
            for pair in range(2):
                cols = slice(pair * LANES, (pair + 1) * LANES)
                q2 = q_ref[0, cls, m0:m0 + B_TQ, cols]
                k2 = kw_ref[cls, m0:m0 + B_TK, cols]
                v2 = jnp.concatenate([vw_ref[cls, m0:m0 + B_TK, cols], ones], axis=1)
                zero = jnp.zeros_like(q2)
                o_h, lse_h = [], []
                for half in range(2):
                    qm = jnp.where(low, q2, zero) if half == 0 else jnp.where(low, zero, q2)
                    s = jnp.where(valid, _dot_nt(qm, k2), NEG_INF)
                    m = jnp.max(s, axis=-1, keepdims=True)
                    ol = _dot(jnp.exp2(s - m).astype(BF16), v2)
                    l = ol[:, LANES:]
                    o_h.append(ol[:, :LANES] / l)
                    lse_h.append(m + jnp.log(l) * LOG2E)
                o_ref[0, cls, m0:m0 + B_TQ, cols] = jnp.where(low, o_h[0], o_h[1])
                lse_ref[0, cls, m0:m0 + B_TQ, cols] = jnp.where(low, lse_h[0], lse_h[1])


def _attn_b_group(q, k, v):
    bsz, dil, n_rows, _ = q.shape
    chunk = min(n_rows, B_CHUNK)
    n_cls = min(dil, B_CHUNK // chunk)
    halo_per_chunk = chunk // B_SIDE
    last_halo = n_rows // B_SIDE - 1

    cur = pl.BlockSpec((1, n_cls, chunk, B_OUT_W), lambda b, r, c: (b, r, c, 0))
    prev = pl.BlockSpec((1, n_cls, B_SIDE, B_OUT_W),
                        lambda b, r, c: (b, r, jnp.maximum(c * halo_per_chunk - 1, 0), 0))
    nxt = pl.BlockSpec((1, n_cls, B_SIDE, B_OUT_W),
                       lambda b, r, c: (b, r, jnp.minimum((c + 1) * halo_per_chunk, last_halo), 0))
    out_sds = jax.ShapeDtypeStruct((bsz, dil, n_rows, B_OUT_W), F32)
    window = pltpu.VMEM((n_cls, chunk + 2 * B_SIDE, B_OUT_W), BF16)
    return pl.pallas_call(
        functools.partial(_attn_b_kernel, n_rows=n_rows),
        grid=(bsz, dil // n_cls, n_rows // chunk),
        in_specs=[cur, prev, cur, nxt, prev, cur, nxt],
        out_specs=[cur, cur],
        out_shape=[out_sds, out_sds],
        scratch_shapes=[window, window],
        compiler_params=_params(3),
        name=f"attn_b_d{dil}",
    )(q, k, k, k, v, v, v)


def _mix_kernel(x_ref, nm_ref, wg_ref, oa_ref, ob0_ref, ob1_ref, ob2_ref,
                l0_ref, l1_ref, l2_ref, wba_ref, wbb_ref, wo_ref, y_ref, scr_ref):
    x = x_ref[0]
    tm = x.shape[0]
    h = _rms(x, nm_ref[...]).astype(BF16)

    def natural_order(slot, blk_ref):
        dil = blk_ref.shape[1]
        if dil == 1:
            return blk_ref[0, 0]
        for s in range(B_OUT_W // LANES):
            for res in range(dil):
                scr_ref[slot, s, pl.ds(res, tm // dil, stride=dil), :] = (
                    blk_ref[0, res, :, s * LANES:(s + 1) * LANES])
        return jnp.concatenate([scr_ref[slot, s] for s in range(B_OUT_W // LANES)], axis=1)

    obs = [natural_order(2 * g, ref) for g, ref in enumerate((ob0_ref, ob1_ref, ob2_ref))]
    l0, l1, l2 = [natural_order(2 * g + 1, ref) for g, ref in enumerate((l0_ref, l1_ref, l2_ref))]
    m = jnp.maximum(jnp.maximum(l0, l1), l2)
    w0, w1, w2 = jnp.exp2(l0 - m), jnp.exp2(l1 - m), jnp.exp2(l2 - m)
    comb = (w0 * obs[0] + w1 * obs[1] + w2 * obs[2]) / (w0 + w1 + w2)
    out_a = _dot(oa_ref[0], wba_ref[...])
    out_b = _dot(comb.astype(BF16), wbb_ref[...])
    ga = _dot(h, wg_ref[:, :D_MODEL])
    gb = _dot(h, wg_ref[:, D_MODEL:])
    merged = jax.nn.sigmoid(ga) * out_a + jax.nn.sigmoid(gb) * out_b
    y_ref[0] = x + _dot(merged.astype(BF16), wo_ref[...])


def _mix(x, nm, wg, oa, obs, lses, wba, wbb, wo):
    bsz, s_len, _ = x.shape
    tm = MIX_TM

    def tok(w):
        return pl.BlockSpec((1, tm, w), lambda b, j: (b, j, 0))

    def classes(a):
        dil = a.shape[1]
        return pl.BlockSpec((1, dil, tm // dil, B_OUT_W), lambda b, j: (b, 0, j, 0))

    return pl.pallas_call(
        _mix_kernel,
        grid=(bsz, s_len // tm),
        in_specs=[tok(D_MODEL), _resident((1, D_MODEL)), _resident((D_MODEL, 2 * D_MODEL)),
                  tok(A_Q_W)] + [classes(a) for a in obs] + [classes(a) for a in lses] +
                 [_resident((A_Q_W, D_MODEL)), _resident((B_OUT_W, D_MODEL)),
                  _resident((D_MODEL, D_MODEL))],
        out_specs=tok(D_MODEL),
        out_shape=jax.ShapeDtypeStruct((bsz, s_len, D_MODEL), F32),
        scratch_shapes=[pltpu.VMEM((2 * len(B_WIN_DIL), B_OUT_W // LANES, tm, LANES), F32)],
        compiler_params=_params(2),
        name="mix",
    )(x, nm, wg, oa, *obs, *lses, wba, wbb, wo)


def _rope_tables(s_len):
    def cos_sin(pos, r, theta):
        inv = theta ** (-jnp.arange(0, r, 2, dtype=F32) / r)
        ang = pos[:, None] * inv[None, :]
        return jnp.cos(ang), jnp.sin(ang)

    t = jnp.arange(s_len, dtype=F32)
    rows = s_len // GRID_W
    row = jnp.repeat(jnp.arange(rows, dtype=F32), GRID_W)
    col = jnp.tile(jnp.arange(GRID_W, dtype=F32), rows)
    half = HEAD_DIM // 2
    cr, sr = cos_sin(row, half, AXIAL_THETA)
    cc, sc = cos_sin(col, half, AXIAL_THETA)
    cos_a = jnp.concatenate([cr, cr, cc, cc], axis=-1)
    sin_a = jnp.concatenate([-sr, sr, -sc, sc], axis=-1)
    cb, sb = cos_sin(t, ROT_DIMS, ROPE_THETA)
    rest = HEAD_DIM - ROT_DIMS
    cos_b = jnp.concatenate([cb, cb, jnp.ones((s_len, rest), F32)], axis=-1)
    sin_b = jnp.concatenate([-sb, sb, jnp.zeros((s_len, rest), F32)], axis=-1)
    return tuple(jnp.tile(x, (1, LANES // HEAD_DIM)) for x in (cos_a, sin_a, cos_b, sin_b))


def _layer(x, p):
    bsz, s_len, _ = x.shape
    x1 = _ffn(x.reshape(bsz * s_len, D_MODEL), p["n1"], p["w13_1"], p["w2_1"])
    x1 = x1.reshape(bsz, s_len, D_MODEL)
    qa, kt, va, *qkv_b = _proj(x1, p["nm"], p["wqkv"], p["qn"], p["kn"], _rope_tables(s_len))
    oa = _attn_a(p["qn"], p["kn"], qa, kt, va)
    obs, lses = [], []
    for g in range(len(B_WIN_DIL)):
        o, lse = _attn_b_group(*qkv_b[3 * g:3 * g + 3])
        obs.append(o)
        lses.append(lse)
    x2 = _mix(x1, p["nm"], p["wg"], oa, obs, lses, p["wba"], p["wbb"], p["wo"])
    return x2.reshape(bsz * s_len, D_MODEL)


def kernel(x_prompt, x_sample, norm_ffn1, w13_ffn1, w2_ffn1, norm_mix, w_in, q_norm_a,
           k_norm_a, w_branch_a, w_branch_b, w_out, norm_ffn2, w13_ffn2, w2_ffn2, norm_final):
    depth = norm_ffn1.shape[0]
    layers = []
    for l in range(depth):
        layers.append(dict(
            n1=norm_ffn1[l][None, :], w13_1=w13_ffn1[l].astype(BF16), w2_1=w2_ffn1[l].astype(BF16),
            nm=norm_mix[l][None, :],
            wqkv=w_in[l, :, :QKV_W].astype(BF16), wg=w_in[l, :, QKV_W:].astype(BF16),
            qn=jnp.tile(q_norm_a[l], LANES // HEAD_DIM)[None, :],
            kn=jnp.tile(k_norm_a[l], LANES // HEAD_DIM)[None, :],
            wba=w_branch_a[l].astype(BF16), wbb=w_branch_b[l].astype(BF16),
            wo=w_out[l].astype(BF16),
            n2=norm_ffn2[l][None, :], w13_2=w13_ffn2[l].astype(BF16), w2_2=w2_ffn2[l].astype(BF16)))
    gf = norm_final[None, :]

    def trunk(x):
        bsz, s_len, _ = x.shape
        n_layers = len(layers)
        for l, p in enumerate(layers):
            x2 = _layer(x, p)
            last = l == n_layers - 1
            x = _ffn(x2, p["n2"], p["w13_2"], p["w2_2"], gf if last else None)
            x = x.reshape(bsz, s_len, D_MODEL)
        return x

    return (trunk(x_prompt), trunk(x_sample))
```

```python
import functools

import jax
import jax.numpy as jnp
from jax import lax
from jax.experimental import pallas as pl
from jax.experimental.pallas import tpu as pltpu

D_MODEL = 1024
HEAD_DIM = 64
A_Q_HEADS = 8
A_KV_HEADS = 2
B_WIN_DIL = ((128, 1), (512, 4), (2048, 16))
B_HEADS = 12
D_FF = 2816
GRID_W = 64
ROPE_THETA = 500000.0
AXIAL_THETA = 10000.0
ROT_DIMS = HEAD_DIM // 4
EPS = 1e-6
NEG_INF = -1e30
A_Q_W = A_Q_HEADS * HEAD_DIM
A_KV_W = A_KV_HEADS * HEAD_DIM
B_W = B_HEADS * HEAD_DIM
B_OUT_W = 4 * HEAD_DIM
QKV_W = A_Q_W + 2 * A_KV_W + 3 * B_W
B_SIDE = 64
LOG2E = 1.4426950408889634
Q_SCALE = HEAD_DIM ** -0.5 * LOG2E
SAFE_LOG2 = 100.0

LANES = 128
VMEM_LIMIT = 56 * 1024 * 1024

BF16 = jnp.bfloat16
F32 = jnp.float32

FFN_TM = 1024
FFN_TF = 256
PROJ_TM = 512
MIX_TM = 512
A_STEP_SCORES = 256 * 4096
B_TQ = 128
B_TK = B_TQ + 2 * B_SIDE
B_CHUNK = 1024


def _dot(a, b):
    return jnp.dot(a, b, preferred_element_type=F32)


def _dot_nt(a, b):
    return lax.dot_general(a, b, (((1,), (1,)), ((), ())), preferred_element_type=F32)


def _rms(x, g):
    ms = jnp.mean(x * x, axis=-1, keepdims=True)
    return x * lax.rsqrt(ms + EPS) * g


def _resident(shape):
    nd = len(shape)
    return pl.BlockSpec(shape, lambda *_: (0,) * nd, pipeline_mode=pl.Buffered(1))


def _params(n_axes):
    return pltpu.CompilerParams(
        dimension_semantics=("arbitrary",) * n_axes, vmem_limit_bytes=VMEM_LIMIT)


def _ffn_kernel(*refs, final_norm):
    if final_norm:
        x_ref, g_ref, w13_ref, w2_ref, gf_ref, o_ref = refs
    else:
        x_ref, g_ref, w13_ref, w2_ref, o_ref = refs
    x = x_ref[...]
    h = _rms(x, g_ref[...]).astype(BF16)
    y = None
    for c in range(D_FF // FFN_TF):
        lo, hi = c * FFN_TF, (c + 1) * FFN_TF
        a = _dot(h, w13_ref[:, lo:hi])
        b = _dot(h, w13_ref[:, D_FF + lo:D_FF + hi])
        t = (a * jax.nn.sigmoid(a) * b).astype(BF16)
        d = _dot(t, w2_ref[lo:hi, :])
        y = d if y is None else y + d
    out = x + 0.5 * y
    if final_norm:
        out = _rms(out, gf_ref[...])
    o_ref[...] = out


def _ffn(x2d, g, w13, w2, gf=None):
    m = x2d.shape[0]
    final_norm = gf is not None
    row = pl.BlockSpec((FFN_TM, D_MODEL), lambda i: (i, 0))
    in_specs = [row, _resident((1, D_MODEL)), _resident((D_MODEL, 2 * D_FF)),
                _resident((D_FF, D_MODEL))]
    args = [x2d, g, w13, w2]
    if final_norm:
        in_specs.append(_resident((1, D_MODEL)))
        args.append(gf)
    return pl.pallas_call(
        functools.partial(_ffn_kernel, final_norm=final_norm),
        grid=(m // FFN_TM,),
        in_specs=in_specs,
        out_specs=row,
        out_shape=jax.ShapeDtypeStruct((m, D_MODEL), F32),
        compiler_params=_params(1),
        name="ffn_final" if final_norm else "ffn",
    )(*args)


def _swap_halves(y, half):
    lane = lax.broadcasted_iota(jnp.int32, y.shape, 1)
    up = pltpu.roll(y, LANES - half, 1)
    dn = pltpu.roll(y, half, 1)
    return jnp.where((lane & half) == 0, up, dn)


def _rope(y, cos, sin, half):
    return y * cos + _swap_halves(y, half) * sin


def _proj_kernel(x_ref, nm_ref, w_ref, qn_ref, kn_ref, ca_ref, sa_ref, cb_ref, sb_ref,
                 qa_ref, kt_ref, va_ref, *rest):
    b_refs, scr_ref = rest[:-1], rest[-1]
    tm = x_ref.shape[1]
    h = _rms(x_ref[0], nm_ref[...]).astype(BF16)
    ca, sa, cb, sb = ca_ref[...], sa_ref[...], cb_ref[...], sb_ref[...]

    r = lax.broadcasted_iota(jnp.int32, (2 * LANES, LANES), 0) % LANES // HEAD_DIM
    c = lax.broadcasted_iota(jnp.int32, (2 * LANES, LANES), 1) // HEAD_DIM
    head_ones = jnp.where(r == c, 1.0, 0.0).astype(BF16)

    def head_norm_rope(xp, gain):
        sq = xp * xp
        sq_hi = sq.astype(BF16)
        sq_lo = (sq - sq_hi.astype(F32)).astype(BF16)
        ss = _dot(jnp.concatenate([sq_hi, sq_lo], axis=1), head_ones)
        y = xp * lax.rsqrt(ss * (1.0 / HEAD_DIM) + EPS) * gain
        return _rope(y, ca, sa, HEAD_DIM // 4)

    def lanes(x, j):
        return x[:, j * LANES:(j + 1) * LANES]

    off = A_Q_W + 2 * A_KV_W
    qkv_b = [_dot(h, w_ref[:, off + i * B_W:off + (i + 1) * B_W]) for i in range(3)]
    for g, (_, dil) in reversed(list(enumerate(B_WIN_DIL))):
        for i, full in enumerate(qkv_b):
            out_ref = b_refs[3 * g + i]
            for s in range(B_OUT_W // LANES):
                blk = lanes(full, g * (B_OUT_W // LANES) + s)
                if i < 2:
                    blk = _rope(blk, cb, sb, ROT_DIMS // 2)
                if i == 0:
                    blk = blk * Q_SCALE
                cols = slice(s * LANES, (s + 1) * LANES)
                if dil == 1:
                    out_ref[0, 0, :, cols] = blk.astype(BF16)
                    continue
                slot = 3 * (g - 1) + i
                scr_ref[slot, s] = blk
                for res in range(dil):
                    rows = scr_ref[slot, s, pl.ds(res, tm // dil, stride=dil), :]
                    out_ref[0, res, :, cols] = rows.astype(BF16)

    qa = _dot(h, w_ref[:, 0:A_Q_W])
    for j in range(A_Q_W // LANES):
        qa_ref[0, :, j * LANES:(j + 1) * LANES] = (
            head_norm_rope(lanes(qa, j), qn_ref[...]) * Q_SCALE).astype(BF16)

    off = A_Q_W
    kva = _dot(h, w_ref[:, off:off + 2 * A_KV_W])
    kt = head_norm_rope(lanes(kva, 0), kn_ref[...]).T.astype(BF16)
    for kv in range(A_KV_HEADS):
        one = kt[kv * HEAD_DIM:(kv + 1) * HEAD_DIM]
        kt_ref[0, kv] = jnp.concatenate([one, one], axis=0)
    va = lanes(kva, 1)
    va_ref[0] = jnp.concatenate([va, jnp.ones_like(va)], axis=1).astype(BF16)


def _proj(x, nm, wqkv, qn, kn, tabs):
    bsz, s_len, _ = x.shape
    tm = PROJ_TM
    tab = pl.BlockSpec((tm, LANES), lambda b, j: (j, 0))

    def tok(w):
        return pl.BlockSpec((1, tm, w), lambda b, j: (b, j, 0))

    out_specs = [tok(A_Q_W),
                 pl.BlockSpec((1, A_KV_HEADS, LANES, tm), lambda b, j: (b, 0, 0, j)),
                 tok(2 * LANES)]
    out_shape = [jax.ShapeDtypeStruct((bsz, s_len, A_Q_W), BF16),
                 jax.ShapeDtypeStruct((bsz, A_KV_HEADS, LANES, s_len), BF16),
                 jax.ShapeDtypeStruct((bsz, s_len, 2 * LANES), BF16)]
    for _, dil in B_WIN_DIL:
        for _ in range(3):
            out_specs.append(pl.BlockSpec((1, dil, tm // dil, B_OUT_W),
                                          lambda b, j: (b, 0, j, 0)))
            out_shape.append(jax.ShapeDtypeStruct((bsz, dil, s_len // dil, B_OUT_W), BF16))
    n_strided = 3 * sum(1 for _, dil in B_WIN_DIL if dil > 1)
    return pl.pallas_call(
        _proj_kernel,
        grid=(bsz, s_len // tm),
        in_specs=[tok(D_MODEL), _resident((1, D_MODEL)), _resident((D_MODEL, QKV_W)),
                  _resident((1, LANES)), _resident((1, LANES)), tab, tab, tab, tab],
        out_specs=out_specs,
        out_shape=out_shape,
        scratch_shapes=[pltpu.VMEM((n_strided, B_OUT_W // LANES, tm, LANES), F32)],
        compiler_params=_params(2),
        name="proj",
    )(x, nm, wqkv, qn, kn, *tabs)


def _attn_a_kernel(qn_ref, kn_ref, q_ref, kt_ref, v_ref, o_ref, safe_ref):
    tq = q_ref.shape[1]
    lane = lax.broadcasted_iota(jnp.int32, (tq, LANES), 1)
    low = lane < HEAD_DIM

    @pl.when((pl.program_id(0) == 0) & (pl.program_id(1) == 0))
    def _():
        gq2 = jnp.max(qn_ref[...] * qn_ref[...])
        gk2 = jnp.max(kn_ref[...] * kn_ref[...])
        bound2 = (HEAD_DIM * Q_SCALE) ** 2 * 1.05 * gq2 * gk2
        safe_ref[0] = (bound2 <= SAFE_LOG2 * SAFE_LOG2).astype(jnp.int32)

    safe = safe_ref[0] == 1

    def head_rows(kv):
        rows = []
        for pair in range(2):
            c0 = kv * 4 * HEAD_DIM + pair * LANES
            q2 = q_ref[0, :, c0:c0 + LANES]
            zero = jnp.zeros_like(q2)
            rows.append(jnp.where(low, q2, zero))
            rows.append(jnp.where(low, zero, q2))
        return rows

    def place(kv, per_head):
        outs = []
        for pair in range(2):
            a, b = per_head[2 * pair], per_head[2 * pair + 1]
            if kv == 0:
                b = pltpu.roll(b, HEAD_DIM, 1)
            else:
                a = pltpu.roll(a, HEAD_DIM, 1)
            outs.append(jnp.where(low, a, b))
        return outs

    @pl.when(safe)
    def _():
        outs = []
        for kv in range(A_KV_HEADS):
            s = _dot(jnp.concatenate(head_rows(kv), axis=0), kt_ref[0, kv])
            ol = _dot(jnp.exp2(s).astype(BF16), v_ref[0])
            o = ol[:, :LANES] / ol[:, LANES:]
            outs += place(kv, [o[g * tq:(g + 1) * tq] for g in range(4)])
        o_ref[0] = jnp.concatenate(outs, axis=1).astype(BF16)

    @pl.when(jnp.logical_not(safe))
    def _():
        outs = []
        for kv in range(A_KV_HEADS):
            per_head = []
            for qm in head_rows(kv):
                s = _dot(qm, kt_ref[0, kv])
                s = s - jnp.max(s, axis=-1, keepdims=True)
                ol = _dot(jnp.exp2(s).astype(BF16), v_ref[0])
                per_head.append(ol[:, :LANES] / ol[:, LANES:])
            outs += place(kv, per_head)
        o_ref[0] = jnp.concatenate(outs, axis=1).astype(BF16)


def _attn_a(qn, kn, qa, kt, va):
    bsz, s_len, _ = qa.shape
    tq = max(LANES, A_STEP_SCORES // s_len)
    return pl.pallas_call(
        _attn_a_kernel,
        grid=(bsz, s_len // tq),
        in_specs=[_resident((1, LANES)), _resident((1, LANES)),
                  pl.BlockSpec((1, tq, A_Q_W), lambda b, i: (b, i, 0)),
                  pl.BlockSpec((1, A_KV_HEADS, LANES, s_len), lambda b, i: (b, 0, 0, 0)),
                  pl.BlockSpec((1, s_len, 2 * LANES), lambda b, i: (b, 0, 0))],
        out_specs=pl.BlockSpec((1, tq, A_Q_W), lambda b, i: (b, i, 0)),
        out_shape=jax.ShapeDtypeStruct((bsz, s_len, A_Q_W), BF16),
        scratch_shapes=[pltpu.SMEM((1,), jnp.int32)],
        compiler_params=_params(2),
        name="attn_a",
    )(qn, kn, qa, kt, va)


def _attn_b_kernel(q_ref, kp_ref, kc_ref, kn_ref, vp_ref, vc_ref, vn_ref, o_ref, lse_ref,
                   kw_ref, vw_ref, *, n_rows):
    n_cls, chunk = q_ref.shape[1], q_ref.shape[2]
    for w_ref, p_ref, c_ref, n_ref in ((kw_ref, kp_ref, kc_ref, kn_ref),
                                       (vw_ref, vp_ref, vc_ref, vn_ref)):
        w_ref[:, 0:B_SIDE] = p_ref[0]
        w_ref[:, B_SIDE:B_SIDE + chunk] = c_ref[0]
        w_ref[:, B_SIDE + chunk:] = n_ref[0]

    lane = lax.broadcasted_iota(jnp.int32, (B_TQ, LANES), 1)
    low = lane < HEAD_DIM
    qi = lax.broadcasted_iota(jnp.int32, (B_TQ, B_TK), 0)
    ki = lax.broadcasted_iota(jnp.int32, (B_TQ, B_TK), 1)
    band = jnp.abs(ki - B_SIDE - qi) <= B_SIDE
    c0 = pl.program_id(2) * chunk
    ones = jnp.ones((B_TK, LANES), BF16)

    n_blk = chunk // B_TQ
    for blk in range(n_blk):
        m0 = blk * B_TQ
        valid = band
        if blk in (0, n_blk - 1):
            kpos = c0 + (m0 - B_SIDE) + ki
            valid = band & (kpos >= 0) & (kpos < n_rows)
        for cls in range(n_cls):
            for pair in range(2):
                cols = slice(pair * LANES, (pair + 1) * LANES)
                q2 = q_ref[0, cls, m0:m0 + B_TQ, cols]
                k2 = kw_ref[cls, m0:m0 + B_TK, cols]
                v2 = jnp.concatenate([vw_ref[cls, m0:m0 + B_TK, cols], ones], axis=1)
                zero = jnp.zeros_like(q2)
                acc_h, l_h, m_h = [], [], []
                for half in range(2):
                    qm = jnp.where(low, q2, zero) if half == 0 else jnp.where(low, zero, q2)
                    s = jnp.where(valid, _dot_nt(qm, k2), NEG_INF)
                    m = jnp.max(s, axis=-1, keepdims=True)
                    ol = _dot(jnp.exp2(s - m).astype(BF16), v2)
                    acc_h.append(ol[:, :LANES])
                    l_h.append(ol[:, LANES:])
                    m_h.append(m)
                l = jnp.where(low, l_h[0], l_h[1])
                o_ref[0, cls, m0:m0 + B_TQ, cols] = jnp.where(low, acc_h[0], acc_h[1]) / l
                lse_ref[0, cls, m0:m0 + B_TQ, cols] = (
                    jnp.where(low, m_h[0], m_h[1]) + jnp.log(l) * LOG2E)


def _attn_b_group(q, k, v):
    bsz, dil, n_rows, _ = q.shape
    chunk = min(n_rows, B_CHUNK)
    n_cls = min(dil, B_CHUNK // chunk)
    halo_per_chunk = chunk // B_SIDE
    last_halo = n_rows // B_SIDE - 1

    cur = pl.BlockSpec((1, n_cls, chunk, B_OUT_W), lambda b, r, c: (b, r, c, 0))
    prev = pl.BlockSpec((1, n_cls, B_SIDE, B_OUT_W),
                        lambda b, r, c: (b, r, jnp.maximum(c * halo_per_chunk - 1, 0), 0))
    nxt = pl.BlockSpec((1, n_cls, B_SIDE, B_OUT_W),
                       lambda b, r, c: (b, r, jnp.minimum((c + 1) * halo_per_chunk, last_halo), 0))
    out_sds = jax.ShapeDtypeStruct((bsz, dil, n_rows, B_OUT_W), F32)
    window = pltpu.VMEM((n_cls, chunk + 2 * B_SIDE, B_OUT_W), BF16)
    return pl.pallas_call(
        functools.partial(_attn_b_kernel, n_rows=n_rows),
        grid=(bsz, dil // n_cls, n_rows // chunk),
        in_specs=[cur, prev, cur, nxt, prev, cur, nxt],
        out_specs=[cur, cur],
        out_shape=[out_sds, out_sds],
        scratch_shapes=[window, window],
        compiler_params=_params(3),
        name=f"attn_b_d{dil}",
    )(q, k, k, k, v, v, v)


def _mix_kernel(x_ref, nm_ref, wg_ref, oa_ref, ob0_ref, ob1_ref, ob2_ref,
                l0_ref, l1_ref, l2_ref, wba_ref, wbb_ref, wo_ref, y_ref, scr_ref):
    x = x_ref[0]
    tm = x.shape[0]
    h = _rms(x, nm_ref[...]).astype(BF16)

    def natural_order(slot, blk_ref):
        dil = blk_ref.shape[1]
        if dil == 1:
            return blk_ref[0, 0]
        for s in range(B_OUT_W // LANES):
            for res in range(dil):
                scr_ref[slot, s, pl.ds(res, tm // dil, stride=dil), :] = (
                    blk_ref[0, res, :, s * LANES:(s + 1) * LANES])
        return jnp.concatenate([scr_ref[slot, s] for s in range(B_OUT_W // LANES)], axis=1)

    obs = [natural_order(2 * g, ref) for g, ref in enumerate((ob0_ref, ob1_ref, ob2_ref))]
    l0, l1, l2 = [natural_order(2 * g + 1, ref) for g, ref in enumerate((l0_ref, l1_ref, l2_ref))]
    m = jnp.maximum(jnp.maximum(l0, l1), l2)
    w0, w1, w2 = jnp.exp2(l0 - m), jnp.exp2(l1 - m), jnp.exp2(l2 - m)
    comb = (w0 * obs[0] + w1 * obs[1] + w2 * obs[2]) / (w0 + w1 + w2)
    out_a = _dot(oa_ref[0], wba_ref[...])
    out_b = _dot(comb.astype(BF16), wbb_ref[...])
    ga = _dot(h, wg_ref[:, :D_MODEL])
    gb = _dot(h, wg_ref[:, D_MODEL:])
    merged = jax.nn.sigmoid(ga) * out_a + jax.nn.sigmoid(gb) * out_b
    y_ref[0] = x + _dot(merged.astype(BF16), wo_ref[...])


def _mix(x, nm, wg, oa, obs, lses, wba, wbb, wo):
    bsz, s_len, _ = x.shape
    tm = MIX_TM

    def tok(w):
        return pl.BlockSpec((1, tm, w), lambda b, j: (b, j, 0))

    def classes(a):
        dil = a.shape[1]
        return pl.BlockSpec((1, dil, tm // dil, B_OUT_W), lambda b, j: (b, 0, j, 0))

    return pl.pallas_call(
        _mix_kernel,
        grid=(bsz, s_len // tm),
        in_specs=[tok(D_MODEL), _resident((1, D_MODEL)), _resident((D_MODEL, 2 * D_MODEL)),
                  tok(A_Q_W)] + [classes(a) for a in obs] + [classes(a) for a in lses] +
                 [_resident((A_Q_W, D_MODEL)), _resident((B_OUT_W, D_MODEL)),
                  _resident((D_MODEL, D_MODEL))],
        out_specs=tok(D_MODEL),
        out_shape=jax.ShapeDtypeStruct((bsz, s_len, D_MODEL), F32),
        scratch_shapes=[pltpu.VMEM((2 * len(B_WIN_DIL), B_OUT_W // LANES, tm, LANES), F32)],
        compiler_params=_params(2),
        name="mix",
    )(x, nm, wg, oa, *obs, *lses, wba, wbb, wo)


def _rope_tables(s_len):
    def cos_sin(pos, r, theta):
        inv = theta ** (-jnp.arange(0, r, 2, dtype=F32) / r)
        ang = pos[:, None] * inv[None, :]
        return jnp.cos(ang), jnp.sin(ang)

    t = jnp.arange(s_len, dtype=F32)
    rows = s_len // GRID_W
    row = jnp.repeat(jnp.arange(rows, dtype=F32), GRID_W)
    col = jnp.tile(jnp.arange(GRID_W, dtype=F32), rows)
    half = HEAD_DIM // 2
    cr, sr = cos_sin(row, half, AXIAL_THETA)
    cc, sc = cos_sin(col, half, AXIAL_THETA)
    cos_a = jnp.concatenate([cr, cr, cc, cc], axis=-1)
    sin_a = jnp.concatenate([-sr, sr, -sc, sc], axis=-1)
    cb, sb = cos_sin(t, ROT_DIMS, ROPE_THETA)
    rest = HEAD_DIM - ROT_DIMS
    cos_b = jnp.concatenate([cb, cb, jnp.ones((s_len, rest), F32)], axis=-1)
    sin_b = jnp.concatenate([-sb, sb, jnp.zeros((s_len, rest), F32)], axis=-1)
    return tuple(jnp.tile(x, (1, LANES // HEAD_DIM)) for x in (cos_a, sin_a, cos_b, sin_b))


def _layer(x, p):
    bsz, s_len, _ = x.shape
    x1 = _ffn(x.reshape(bsz * s_len, D_MODEL), p["n1"], p["w13_1"], p["w2_1"])
    x1 = x1.reshape(bsz, s_len, D_MODEL)
    qa, kt, va, *qkv_b = _proj(x1, p["nm"], p["wqkv"], p["qn"], p["kn"], _rope_tables(s_len))
    oa = _attn_a(p["qn"], p["kn"], qa, kt, va)
    obs, lses = [], []
    for g in range(len(B_WIN_DIL)):
        o, lse = _attn_b_group(*qkv_b[3 * g:3 * g + 3])
        obs.append(o)
        lses.append(lse)
    x2 = _mix(x1, p["nm"], p["wg"], oa, obs, lses, p["wba"], p["wbb"], p["wo"])
    return x2.reshape(bsz * s_len, D_MODEL)


def kernel(x_prompt, x_sample, norm_ffn1, w13_ffn1, w2_ffn1, norm_mix, w_in, q_norm_a,
           k_norm_a, w_branch_a, w_branch_b, w_out, norm_ffn2, w13_ffn2, w2_ffn2, norm_final):
    depth = norm_ffn1.shape[0]
    layers = []
    for l in range(depth):
        layers.append(dict(
            n1=norm_ffn1[l][None, :], w13_1=w13_ffn1[l].astype(BF16), w2_1=w2_ffn1[l].astype(BF16),
            nm=norm_mix[l][None, :],
            wqkv=w_in[l, :, :QKV_W].astype(BF16), wg=w_in[l, :, QKV_W:].astype(BF16),
            qn=jnp.tile(q_norm_a[l], LANES // HEAD_DIM)[None, :],
            kn=jnp.tile(k_norm_a[l], LANES // HEAD_DIM)[None, :],
            wba=w_branch_a[l].astype(BF16), wbb=w_branch_b[l].astype(BF16),
            wo=w_out[l].astype(BF16),
            n2=norm_ffn2[l][None, :], w13_2=w13_ffn2[l].astype(BF16), w2_2=w2_ffn2[l].astype(BF16)))
    gf = norm_final[None, :]

    def trunk(x):
        bsz, s_len, _ = x.shape
        n_layers = len(layers)
        for l, p in enumerate(layers):
            x2 = _layer(x, p)
            last = l == n_layers - 1
            x = _ffn(x2, p["n2"], p["w13_2"], p["w2_2"], gf if last else None)
            x = x.reshape(bsz, s_len, D_MODEL)
        return x

    return (trunk(x_prompt), trunk(x_sample))
```

```python
import functools

import jax
import jax.numpy as jnp
from jax import lax
from jax.experimental import pallas as pl
from jax.experimental.pallas import tpu as pltpu

D_MODEL = 1024
HEAD_DIM = 64
A_Q_HEADS = 8
A_KV_HEADS = 2
B_WIN_DIL = ((128, 1), (512, 4), (2048, 16))
B_HEADS = 12
D_FF = 2816
GRID_W = 64
ROPE_THETA = 500000.0
AXIAL_THETA = 10000.0
ROT_DIMS = HEAD_DIM // 4
EPS = 1e-6
NEG_INF = -1e30
A_Q_W = A_Q_HEADS * HEAD_DIM
A_KV_W = A_KV_HEADS * HEAD_DIM
B_W = B_HEADS * HEAD_DIM
B_OUT_W = 4 * HEAD_DIM
QKV_W = A_Q_W + 2 * A_KV_W + 3 * B_W
B_SIDE = 64
LOG2E = 1.4426950408889634
Q_SCALE = HEAD_DIM ** -0.5 * LOG2E
SAFE_LOG2 = 100.0

LANES = 128
VMEM_LIMIT = 56 * 1024 * 1024

BF16 = jnp.bfloat16
F32 = jnp.float32

FFN_TM = 1024
FFN_TF = 256
PROJ_TM = 1024
MIX_TM = 1024
A_STEP_SCORES = 256 * 4096
B_TQ = 128
B_TK = B_TQ + 2 * B_SIDE
B_CHUNK = 4096


def _dot(a, b):
    return jnp.dot(a, b, preferred_element_type=F32)


def _dot_nt(a, b):
    return lax.dot_general(a, b, (((1,), (1,)), ((), ())), preferred_element_type=F32)


def _rms(x, g):
    ms = jnp.mean(x * x, axis=-1, keepdims=True)
    return x * lax.rsqrt(ms + EPS) * g


def _resident(shape):
    nd = len(shape)
    return pl.BlockSpec(shape, lambda *_: (0,) * nd, pipeline_mode=pl.Buffered(1))


def _params(n_axes):
    return pltpu.CompilerParams(
        dimension_semantics=("arbitrary",) * n_axes, vmem_limit_bytes=VMEM_LIMIT)


def _ffn_kernel(*refs, final_norm):
    if final_norm:
        x_ref, g_ref, w13_ref, w2_ref, gf_ref, o_ref = refs
    else:
        x_ref, g_ref, w13_ref, w2_ref, o_ref = refs
    x = x_ref[...]
    h = _rms(x, g_ref[...]).astype(BF16)
    y = None
    for c in range(D_FF // FFN_TF):
        lo, hi = c * FFN_TF, (c + 1) * FFN_TF
        a = _dot(h, w13_ref[:, lo:hi])
        b = _dot(h, w13_ref[:, D_FF + lo:D_FF + hi])
        t = (a * jax.nn.sigmoid(a) * b).astype(BF16)
        d = _dot(t, w2_ref[lo:hi, :])
        y = d if y is None else y + d
    out = x + 0.5 * y
    if final_norm:
        out = _rms(out, gf_ref[...])
    o_ref[...] = out


def _ffn(x2d, g, w13, w2, gf=None):
    m = x2d.shape[0]
    final_norm = gf is not None
    row = pl.BlockSpec((FFN_TM, D_MODEL), lambda i: (i, 0))
    in_specs = [row, _resident((1, D_MODEL)), _resident((D_MODEL, 2 * D_FF)),
                _resident((D_FF, D_MODEL))]
    args = [x2d, g, w13, w2]
    if final_norm:
        in_specs.append(_resident((1, D_MODEL)))
        args.append(gf)
    return pl.pallas_call(
        functools.partial(_ffn_kernel, final_norm=final_norm),
        grid=(m // FFN_TM,),
        in_specs=in_specs,
        out_specs=row,
        out_shape=jax.ShapeDtypeStruct((m, D_MODEL), F32),
        compiler_params=_params(1),
        name="ffn_final" if final_norm else "ffn",
    )(*args)


def _swap_halves(y, half):
    lane = lax.broadcasted_iota(jnp.int32, y.shape, 1)
    up = pltpu.roll(y, LANES - half, 1)
    dn = pltpu.roll(y, half, 1)
    return jnp.where((lane & half) == 0, up, dn)


def _rope(y, cos, sin, half):
    return y * cos + _swap_halves(y, half) * sin


def _proj_kernel(x_ref, nm_ref, w_ref, qn_ref, kn_ref, ca_ref, sa_ref, cb_ref, sb_ref,
                 qa_ref, kt_ref, va_ref, *rest):
    b_refs, scr_ref = rest[:-1], rest[-1]
    tm = x_ref.shape[1]
    h = _rms(x_ref[0], nm_ref[...]).astype(BF16)
    ca, sa, cb, sb = ca_ref[...], sa_ref[...], cb_ref[...], sb_ref[...]

    r = lax.broadcasted_iota(jnp.int32, (2 * LANES, LANES), 0) % LANES // HEAD_DIM
    c = lax.broadcasted_iota(jnp.int32, (2 * LANES, LANES), 1) // HEAD_DIM
    head_ones = jnp.where(r == c, 1.0, 0.0).astype(BF16)

    def head_norm_rope(xp, gain):
        sq = xp * xp
        sq_hi = sq.astype(BF16)
        sq_lo = (sq - sq_hi.astype(F32)).astype(BF16)
        ss = _dot(jnp.concatenate([sq_hi, sq_lo], axis=1), head_ones)
        y = xp * lax.rsqrt(ss * (1.0 / HEAD_DIM) + EPS) * gain
        return _rope(y, ca, sa, HEAD_DIM // 4)

    def lanes(x, j):
        return x[:, j * LANES:(j + 1) * LANES]

    off = A_Q_W + 2 * A_KV_W
    qkv_b = [_dot(h, w_ref[:, off + i * B_W:off + (i + 1) * B_W]) for i in range(3)]
    for g, (_, dil) in reversed(list(enumerate(B_WIN_DIL))):
        for i, full in enumerate(qkv_b):
            out_ref = b_refs[3 * g + i]
            for s in range(B_OUT_W // LANES):
                blk = lanes(full, g * (B_OUT_W // LANES) + s)
                if i < 2:
                    blk = _rope(blk, cb, sb, ROT_DIMS // 2)
                if i == 0:
                    blk = blk * Q_SCALE
                cols = slice(s * LANES, (s + 1) * LANES)
                if dil == 1:
                    out_ref[0, 0, :, cols] = blk.astype(BF16)
                    continue
                slot = 3 * (g - 1) + i
                scr_ref[slot, s] = blk
                for res in range(dil):
                    rows = scr_ref[slot, s, pl.ds(res, tm // dil, stride=dil), :]
                    out_ref[0, res, :, cols] = rows.astype(BF16)

    qa = _dot(h, w_ref[:, 0:A_Q_W])
    for j in range(A_Q_W // LANES):
        qa_ref[0, :, j * LANES:(j + 1) * LANES] = (
            head_norm_rope(lanes(qa, j), qn_ref[...]) * Q_SCALE).astype(BF16)

    off = A_Q_W
    kva = _dot(h, w_ref[:, off:off + 2 * A_KV_W])
    kt = head_norm_rope(lanes(kva, 0), kn_ref[...]).T.astype(BF16)
    for kv in range(A_KV_HEADS):
        one = kt[kv * HEAD_DIM:(kv + 1) * HEAD_DIM]
        kt_ref[0, kv] = jnp.concatenate([one, one], axis=0)
    va = lanes(kva, 1)
    va_ref[0] = jnp.concatenate([va, jnp.ones_like(va)], axis=1).astype(BF16)


def _proj(x, nm, wqkv, qn, kn, tabs):
    bsz, s_len, _ = x.shape
    tm = PROJ_TM
    tab = pl.BlockSpec((tm, LANES), lambda b, j: (j, 0))

    def tok(w):
        return pl.BlockSpec((1, tm, w), lambda b, j: (b, j, 0))

    out_specs = [tok(A_Q_W),
                 pl.BlockSpec((1, A_KV_HEADS, LANES, tm), lambda b, j: (b, 0, 0, j)),
                 tok(2 * LANES)]
    out_shape = [jax.ShapeDtypeStruct((bsz, s_len, A_Q_W), BF16),
                 jax.ShapeDtypeStruct((bsz, A_KV_HEADS, LANES, s_len), BF16),
                 jax.ShapeDtypeStruct((bsz, s_len, 2 * LANES), BF16)]
    for _, dil in B_WIN_DIL:
        for _ in range(3):
            out_specs.append(pl.BlockSpec((1, dil, tm // dil, B_OUT_W),
                                          lambda b, j: (b, 0, j, 0)))
            out_shape.append(jax.ShapeDtypeStruct((bsz, dil, s_len // dil, B_OUT_W), BF16))
    n_strided = 3 * sum(1 for _, dil in B_WIN_DIL if dil > 1)
    return pl.pallas_call(
        _proj_kernel,
        grid=(bsz, s_len // tm),
        in_specs=[tok(D_MODEL), _resident((1, D_MODEL)), _resident((D_MODEL, QKV_W)),
                  _resident((1, LANES)), _resident((1, LANES)), tab, tab, tab, tab],
        out_specs=out_specs,
        out_shape=out_shape,
        scratch_shapes=[pltpu.VMEM((n_strided, B_OUT_W // LANES, tm, LANES), F32)],
        compiler_params=_params(2),
        name="proj",
    )(x, nm, wqkv, qn, kn, *tabs)


def _attn_a_kernel(qn_ref, kn_ref, q_ref, kt_ref, v_ref, o_ref, safe_ref):
    tq = q_ref.shape[1]
    lane = lax.broadcasted_iota(jnp.int32, (tq, LANES), 1)
    low = lane < HEAD_DIM

    @pl.when((pl.program_id(0) == 0) & (pl.program_id(1) == 0))
    def _():
        gq2 = jnp.max(qn_ref[...] * qn_ref[...])
        gk2 = jnp.max(kn_ref[...] * kn_ref[...])
        bound2 = (HEAD_DIM * Q_SCALE) ** 2 * 1.05 * gq2 * gk2
        safe_ref[0] = (bound2 <= SAFE_LOG2 * SAFE_LOG2).astype(jnp.int32)

    safe = safe_ref[0] == 1

    def head_rows(kv):
        rows = []
        for pair in range(2):
            c0 = kv * 4 * HEAD_DIM + pair * LANES
            q2 = q_ref[0, :, c0:c0 + LANES]
            zero = jnp.zeros_like(q2)
            rows.append(jnp.where(low, q2, zero))
            rows.append(jnp.where(low, zero, q2))
        return rows

    def place(kv, per_head):
        outs = []
        for pair in range(2):
            a, b = per_head[2 * pair], per_head[2 * pair + 1]
            if kv == 0:
                b = pltpu.roll(b, HEAD_DIM, 1)
            else:
                a = pltpu.roll(a, HEAD_DIM, 1)
            outs.append(jnp.where(low, a, b))
        return outs

    @pl.when(safe)
    def _():
        outs = []
        for kv in range(A_KV_HEADS):
            s = _dot(jnp.concatenate(head_rows(kv), axis=0), kt_ref[0, kv])
            ol = _dot(jnp.exp2(s).astype(BF16), v_ref[0])
            o = ol[:, :LANES] / ol[:, LANES:]
            outs += place(kv, [o[g * tq:(g + 1) * tq] for g in range(4)])
        o_ref[0] = jnp.concatenate(outs, axis=1).astype(BF16)

    @pl.when(jnp.logical_not(safe))
    def _():
        outs = []
        for kv in range(A_KV_HEADS):
            per_head = []
            for qm in head_rows(kv):
                s = _dot(qm, kt_ref[0, kv])
                s = s - jnp.max(s, axis=-1, keepdims=True)
                ol = _dot(jnp.exp2(s).astype(BF16), v_ref[0])
                per_head.append(ol[:, :LANES] / ol[:, LANES:])
            outs += place(kv, per_head)
        o_ref[0] = jnp.concatenate(outs, axis=1).astype(BF16)


def _attn_a(qn, kn, qa, kt, va):
    bsz, s_len, _ = qa.shape
    tq = max(LANES, A_STEP_SCORES // s_len)
    return pl.pallas_call(
        _attn_a_kernel,
        grid=(bsz, s_len // tq),
        in_specs=[_resident((1, LANES)), _resident((1, LANES)),
                  pl.BlockSpec((1, tq, A_Q_W), lambda b, i: (b, i, 0)),
                  pl.BlockSpec((1, A_KV_HEADS, LANES, s_len), lambda b, i: (b, 0, 0, 0)),
                  pl.BlockSpec((1, s_len, 2 * LANES), lambda b, i: (b, 0, 0))],
        out_specs=pl.BlockSpec((1, tq, A_Q_W), lambda b, i: (b, i, 0)),
        out_shape=jax.ShapeDtypeStruct((bsz, s_len, A_Q_W), BF16),
        scratch_shapes=[pltpu.SMEM((1,), jnp.int32)],
        compiler_params=_params(2),
        name="attn_a",
    )(qn, kn, qa, kt, va)


def _attn_b_kernel(q_ref, kp_ref, kc_ref, kn_ref, vp_ref, vc_ref, vn_ref, o_ref, lse_ref,
                   kw_ref, vw_ref, *, n_rows):
    n_cls, chunk = q_ref.shape[1], q_ref.shape[2]
    for w_ref, p_ref, c_ref, n_ref in ((kw_ref, kp_ref, kc_ref, kn_ref),
                                       (vw_ref, vp_ref, vc_ref, vn_ref)):
        w_ref[:, 0:B_SIDE] = p_ref[0]
        w_ref[:, B_SIDE:B_SIDE + chunk] = c_ref[0]
        w_ref[:, B_SIDE + chunk:] = n_ref[0]

    lane = lax.broadcasted_iota(jnp.int32, (B_TQ, LANES), 1)
    low = lane < HEAD_DIM
    qi = lax.broadcasted_iota(jnp.int32, (2 * B_TQ, B_TK), 0) & (B_TQ - 1)
    ki = lax.broadcasted_iota(jnp.int32, (2 * B_TQ, B_TK), 1)
    band = jnp.abs(ki - B_SIDE - qi) <= B_SIDE
    c0 = pl.program_id(2) * chunk
    ones = jnp.ones((B_TK, LANES), BF16)

    n_blk = chunk // B_TQ
    for blk in range(n_blk):
        m0 = blk * B_TQ
        valid = band
        if blk in (0, n_blk - 1):
            kpos = c0 + (m0 - B_SIDE) + ki
            valid = band & (kpos >= 0) & (kpos < n_rows)
        for cls in range(n_cls):
            for pair in range(2):
                cols = slice(pair * LANES, (pair + 1) * LANES)
                q2 = q_ref[0, cls, m0:m0 + B_TQ, cols]
                k2 = kw_ref[cls, m0:m0 + B_TK, cols]
                v2 = jnp.concatenate([vw_ref[cls, m0:m0 + B_TK, cols], ones], axis=1)
                zero = jnp.zeros_like(q2)
                qs = jnp.concatenate([jnp.where(low, q2, zero), jnp.where(low, zero, q2)], axis=0)
                s = jnp.where(valid, _dot_nt(qs, k2), NEG_INF)
                m = jnp.max(s, axis=-1, keepdims=True)
                ol = _dot(jnp.exp2(s - m).astype(BF16), v2)
                top, bot = ol[:B_TQ], ol[B_TQ:]
                l = jnp.where(low, top[:, LANES:], bot[:, LANES:])
                o_ref[0, cls, m0:m0 + B_TQ, cols] = (
                    jnp.where(low, top[:, :LANES], bot[:, :LANES]) / l)
                lse_ref[0, cls, m0:m0 + B_TQ, cols] = (
                    jnp.where(low, m[:B_TQ], m[B_TQ:]) + jnp.log(l) * LOG2E)


def _attn_b_group(q, k, v):
    bsz, dil, n_rows, _ = q.shape
    chunk = min(n_rows, B_CHUNK)
    n_cls = min(dil, B_CHUNK // chunk)
    halo_per_chunk = chunk // B_SIDE
    last_halo = n_rows // B_SIDE - 1

    cur = pl.BlockSpec((1, n_cls, chunk, B_OUT_W), lambda b, r, c: (b, r, c, 0))
    prev = pl.BlockSpec((1, n_cls, B_SIDE, B_OUT_W),
                        lambda b, r, c: (b, r, jnp.maximum(c * halo_per_chunk - 1, 0), 0))
    nxt = pl.BlockSpec((1, n_cls, B_SIDE, B_OUT_W),
                       lambda b, r, c: (b, r, jnp.minimum((c + 1) * halo_per_chunk, last_halo), 0))
    out_sds = jax.ShapeDtypeStruct((bsz, dil, n_rows, B_OUT_W), F32)
    window = pltpu.VMEM((n_cls, chunk + 2 * B_SIDE, B_OUT_W), BF16)
    return pl.pallas_call(
        functools.partial(_attn_b_kernel, n_rows=n_rows),
        grid=(bsz, dil // n_cls, n_rows // chunk),
        in_specs=[cur, prev, cur, nxt, prev, cur, nxt],
        out_specs=[cur, cur],
        out_shape=[out_sds, out_sds],
        scratch_shapes=[window, window],
        compiler_params=_params(3),
        name=f"attn_b_d{dil}",
    )(q, k, k, k, v, v, v)


def _mix_kernel(x_ref, nm_ref, wg_ref, oa_ref, ob0_ref, ob1_ref, ob2_ref,
                l0_ref, l1_ref, l2_ref, wba_ref, wbb_ref, wo_ref, y_ref, scr_ref):
    x = x_ref[0]
    tm = x.shape[0]
    h = _rms(x, nm_ref[...]).astype(BF16)

    def natural_order(slot, blk_ref):
        dil = blk_ref.shape[1]
        if dil == 1:
            return blk_ref[0, 0]
        for s in range(B_OUT_W // LANES):
            for res in range(dil):
                scr_ref[slot, s, pl.ds(res, tm // dil, stride=dil), :] = (
                    blk_ref[0, res, :, s * LANES:(s + 1) * LANES])
        return jnp.concatenate([scr_ref[slot, s] for s in range(B_OUT_W // LANES)], axis=1)

    obs = [natural_order(2 * g, ref) for g, ref in enumerate((ob0_ref, ob1_ref, ob2_ref))]
    l0, l1, l2 = [natural_order(2 * g + 1, ref) for g, ref in enumerate((l0_ref, l1_ref, l2_ref))]
    m = jnp.maximum(jnp.maximum(l0, l1), l2)
    w0, w1, w2 = jnp.exp2(l0 - m), jnp.exp2(l1 - m), jnp.exp2(l2 - m)
    comb = (w0 * obs[0] + w1 * obs[1] + w2 * obs[2]) / (w0 + w1 + w2)
    out_a = _dot(oa_ref[0], wba_ref[...])
    out_b = _dot(comb.astype(BF16), wbb_ref[...])
    ga = _dot(h, wg_ref[:, :D_MODEL])
    gb = _dot(h, wg_ref[:, D_MODEL:])
    merged = jax.nn.sigmoid(ga) * out_a + jax.nn.sigmoid(gb) * out_b
    y_ref[0] = x + _dot(merged.astype(BF16), wo_ref[...])


def _mix(x, nm, wg, oa, obs, lses, wba, wbb, wo):
    bsz, s_len, _ = x.shape
    tm = MIX_TM

    def tok(w):
        return pl.BlockSpec((1, tm, w), lambda b, j: (b, j, 0))

    def classes(a):
        dil = a.shape[1]
        return pl.BlockSpec((1, dil, tm // dil, B_OUT_W), lambda b, j: (b, 0, j, 0))

    return pl.pallas_call(
        _mix_kernel,
        grid=(bsz, s_len // tm),
        in_specs=[tok(D_MODEL), _resident((1, D_MODEL)), _resident((D_MODEL, 2 * D_MODEL)),
                  tok(A_Q_W)] + [classes(a) for a in obs] + [classes(a) for a in lses] +
                 [_resident((A_Q_W, D_MODEL)), _resident((B_OUT_W, D_MODEL)),
                  _resident((D_MODEL, D_MODEL))],
        out_specs=tok(D_MODEL),
        out_shape=jax.ShapeDtypeStruct((bsz, s_len, D_MODEL), F32),
        scratch_shapes=[pltpu.VMEM((2 * len(B_WIN_DIL), B_OUT_W // LANES, tm, LANES), F32)],
        compiler_params=_params(2),
        name="mix",
    )(x, nm, wg, oa, *obs, *lses, wba, wbb, wo)


def _rope_tables(s_len):
    def cos_sin(pos, r, theta):
        inv = theta ** (-jnp.arange(0, r, 2, dtype=F32) / r)
        ang = pos[:, None] * inv[None, :]
        return jnp.cos(ang), jnp.sin(ang)

    t = jnp.arange(s_len, dtype=F32)
    rows = s_len // GRID_W
    row = jnp.repeat(jnp.arange(rows, dtype=F32), GRID_W)
    col = jnp.tile(jnp.arange(GRID_W, dtype=F32), rows)
    half = HEAD_DIM // 2
    cr, sr = cos_sin(row, half, AXIAL_THETA)
    cc, sc = cos_sin(col, half, AXIAL_THETA)
    cos_a = jnp.concatenate([cr, cr, cc, cc], axis=-1)
    sin_a = jnp.concatenate([-sr, sr, -sc, sc], axis=-1)
    cb, sb = cos_sin(t, ROT_DIMS, ROPE_THETA)
    rest = HEAD_DIM - ROT_DIMS
    cos_b = jnp.concatenate([cb, cb, jnp.ones((s_len, rest), F32)], axis=-1)
    sin_b = jnp.concatenate([-sb, sb, jnp.zeros((s_len, rest), F32)], axis=-1)
    return tuple(jnp.tile(x, (1, LANES // HEAD_DIM)) for x in (cos_a, sin_a, cos_b, sin_b))


def _layer(x, p):
    bsz, s_len, _ = x.shape
    x1 = _ffn(x.reshape(bsz * s_len, D_MODEL), p["n1"], p["w13_1"], p["w2_1"])
    x1 = x1.reshape(bsz, s_len, D_MODEL)
    qa, kt, va, *qkv_b = _proj(x1, p["nm"], p["wqkv"], p["qn"], p["kn"], _rope_tables(s_len))
    oa = _attn_a(p["qn"], p["kn"], qa, kt, va)
    obs, lses = [], []
    for g in range(len(B_WIN_DIL)):
        o, lse = _attn_b_group(*qkv_b[3 * g:3 * g + 3])
        obs.append(o)
        lses.append(lse)
    x2 = _mix(x1, p["nm"], p["wg"], oa, obs, lses, p["wba"], p["wbb"], p["wo"])
    return x2.reshape(bsz * s_len, D_MODEL)


def kernel(x_prompt, x_sample, norm_ffn1, w13_ffn1, w2_ffn1, norm_mix, w_in, q_norm_a,
           k_norm_a, w_branch_a, w_branch_b, w_out, norm_ffn2, w13_ffn2, w2_ffn2, norm_final):
    depth = norm_ffn1.shape[0]
    layers = []
    for l in range(depth):
        layers.append(dict(
            n1=norm_ffn1[l][None, :], w13_1=w13_ffn1[l].astype(BF16), w2_1=w2_ffn1[l].astype(BF16),
            nm=norm_mix[l][None, :],
            wqkv=w_in[l, :, :QKV_W].astype(BF16), wg=w_in[l, :, QKV_W:].astype(BF16),
            qn=jnp.tile(q_norm_a[l], LANES // HEAD_DIM)[None, :],
            kn=jnp.tile(k_norm_a[l], LANES // HEAD_DIM)[None, :],
            wba=w_branch_a[l].astype(BF16), wbb=w_branch_b[l].astype(BF16),
            wo=w_out[l].astype(BF16),
            n2=norm_ffn2[l][None, :], w13_2=w13_ffn2[l].astype(BF16), w2_2=w2_ffn2[l].astype(BF16)))
    gf = norm_final[None, :]

    def trunk(x):
        bsz, s_len, _ = x.shape
        n_layers = len(layers)
        for l, p in enumerate(layers):
            x2 = _layer(x, p)
            last = l == n_layers - 1
            x = _ffn(x2, p["n2"], p["w13_2"], p["w2_2"], gf if last else None)
            x = x.reshape(bsz, s_len, D_MODEL)
        return x

    return (trunk(x_prompt), trunk(x_sample))
```
